```python
import jax, jax.numpy as jnp
from jax import lax
import numpy as np

D_MODEL = 1024
BATCH = 8
SEQ = 2048
DEPTH = 4
DEC_BATCH = 8
DEC_SEQ = 32
PAST_LEN = 4096

CHUNK = 64
Q_BLOCK = 128
MLA_HEADS = 8
MLA_D_NOPE = 64
MLA_D_ROPE = 32
MLA_D_V = 64
MLA_Q_RANK = 256
MLA_KV_RANK = 256
MLA_SCALE = (MLA_D_NOPE + MLA_D_ROPE) ** -0.5
ROPE_THETA = 10000.0
MASK_VALUE = -1e30
HG_HEADS = 4
HG_DK = 128
HG_DV = 128
LB_TINY = 1e-30
CM_CHUNK = 128
CM_GROUPS = 4
CM_WIDTH = 512
CM_GROUP_DIM = CM_WIDTH // CM_GROUPS
D_FF = -(-8 * D_MODEL // (3 * 256)) * 256
ALPHA = (2 * DEPTH) ** 0.25
BETA = (8 * DEPTH) ** -0.25
EPS = 1e-5

SPLIT_SIZES = (MLA_Q_RANK, MLA_KV_RANK, MLA_D_ROPE,
               HG_HEADS * HG_DK, HG_HEADS * HG_DK, HG_HEADS * HG_DV, HG_HEADS * HG_DV,
               CM_WIDTH, CM_WIDTH, D_MODEL, D_MODEL, D_MODEL)
D_IN = sum(SPLIT_SIZES)

kernel_name = "hybrid_mla_hgrn2_chunkmlp_stream_step"


def layer_norm(x, g, b):
    xf = x.astype(jnp.float32)
    mu = jnp.mean(xf, -1, keepdims=True)
    var = jnp.mean(jnp.square(xf - mu), -1, keepdims=True)
    return ((xf - mu) * lax.rsqrt(var + EPS) * g + b).astype(x.dtype)


def rms_norm(x, g):
    xf = x.astype(jnp.float32)
    return (xf * lax.rsqrt(jnp.mean(xf * xf, -1, keepdims=True) + EPS) * g).astype(x.dtype)


def rope(x, pos):
    half = x.shape[-1] // 2
    inv = 1.0 / (ROPE_THETA ** (jnp.arange(half, dtype=jnp.float32) / half))
    ang = pos.astype(jnp.float32)[:, None] * inv[None]
    cos = jnp.cos(ang)[None, :, None]
    sin = jnp.sin(ang)[None, :, None]
    xf = x.astype(jnp.float32)
    x1, x2 = xf[..., :half], xf[..., half:]
    return jnp.concatenate([x1 * cos - x2 * sin, x2 * cos + x1 * sin], -1).astype(x.dtype)


def mla_attend(q_lat, q_rope, ckv, krope, q_pos, k_pos):
    s = jnp.einsum('bqhr,bkr->bhqk', q_lat, ckv) + jnp.einsum('bqhd,bkd->bhqk', q_rope, krope)
    s = s.astype(jnp.float32) * MLA_SCALE
    mask = (k_pos[None, :] // CHUNK) <= (q_pos[:, None] // CHUNK)
    s = jnp.where(mask[None, None], s, MASK_VALUE)
    p = jax.nn.softmax(s, axis=-1).astype(ckv.dtype)
    return jnp.einsum('bhqk,bkr->bqhr', p, ckv)


def mla_prompt(q_lat, q_rope, ckv, krope, pos):
    B, S = q_lat.shape[:2]
    nb = S // Q_BLOCK
    ql = q_lat.reshape(B, nb, Q_BLOCK, MLA_HEADS, MLA_KV_RANK).swapaxes(0, 1)
    qr = q_rope.reshape(B, nb, Q_BLOCK, MLA_HEADS, MLA_D_ROPE).swapaxes(0, 1)
    qp = pos.reshape(nb, Q_BLOCK)
    o = lax.map(lambda a: mla_attend(a[0], a[1], ckv, krope, a[2], pos), (ql, qr, qp))
    return o.swapaxes(0, 1).reshape(B, S, MLA_HEADS, MLA_KV_RANK)


def hgrn_chunk(S0, q, k, v, logf):
    L = q.shape[1]
    b = jnp.cumsum(logf, axis=1)
    causal = jnp.tril(jnp.ones((L, L), dtype=bool))[None, :, :, None, None]
    diff = b[:, :, None] - b[:, None, :]
    decay = jnp.where(causal, jnp.exp(jnp.where(causal, diff, 0.0)), 0.0)
    A = jnp.einsum('bthk,btshk->btsh', q, decay * k[:, None])
    o = jnp.einsum('btsh,bshv->bthv', A, v) + jnp.einsum('bthk,bhkv->bthv', q * jnp.exp(b), S0)
    bL = b[:, -1]
    S_new = jnp.exp(bL)[..., None] * S0 + jnp.einsum('bshk,bshv->bhkv', k * jnp.exp(bL[:, None] - b), v)
    return S_new, o


def hgrn_prompt(q, k, v, logf):
    B, S = q.shape[:2]
    n = S // CHUNK
    to_c = lambda t: t.reshape(B, n, CHUNK, *t.shape[2:]).swapaxes(0, 1)
    S0 = jnp.zeros((B, HG_HEADS, HG_DK, HG_DV), jnp.float32)
    S_fin, o = lax.scan(lambda s, xs: hgrn_chunk(s, *xs), S0, (to_c(q), to_c(k), to_c(v), to_c(logf)))
    return S_fin, o.swapaxes(0, 1).reshape(B, S, HG_HEADS, HG_DV)


def chunk_mlp(u, v, w_s, b_s):
    B, S, _ = v.shape
    L = min(S, CM_CHUNK)
    n = S // L
    tri = jnp.tril(w_s[:, :L, :L])
    vc = v.reshape(B, n, L, CM_GROUPS, CM_GROUP_DIM)
    s = jnp.einsum('gij,bnjgc->bnigc', tri, vc) + b_s[:, :L].T[None, None, :, :, None]
    return u * s.reshape(B, S, CM_WIDTH)


def trunk_layer(x, pos, lw, lb, past):
    B, S, _ = x.shape
    idx = np.cumsum(SPLIT_SIZES)[:-1].tolist()
    cq, ckv_raw, kr_raw, hq, hf, hi, hg, cu, cv, ga, gb, gc = jnp.split(x @ lw['w_in'], idx, axis=-1)

    cq = rms_norm(cq, lw['q_norm_g'])
    q = (cq @ lw['w_uq']).reshape(B, S, MLA_HEADS, MLA_D_NOPE + MLA_D_ROPE)
    q_rope = rope(q[..., MLA_D_NOPE:], pos)
    q_lat = jnp.einsum('bshn,rhn->bshr', q[..., :MLA_D_NOPE], lw['w_uk'])
    ckv = rms_norm(ckv_raw, lw['kv_norm_g'])
    krope = rope(kr_raw[:, :, None], pos)[:, :, 0]

    lbh = lb.reshape(HG_HEADS, HG_DK)
    zf = hf.astype(jnp.float32).reshape(B, S, HG_HEADS, HG_DK)
    logf = jnp.logaddexp(jnp.log(lbh + LB_TINY), jnp.log1p(-lbh) + jax.nn.log_sigmoid(zf))
    kk = (1.0 - lbh) * jax.nn.sigmoid(-zf)
    qq = hq.astype(jnp.float32).reshape(B, S, HG_HEADS, HG_DK)
    vv = hi.astype(jnp.float32).reshape(B, S, HG_HEADS, HG_DV)

    if past is None:
        o_lat = mla_prompt(q_lat, q_rope, ckv, krope, pos)
        S_fin, o_h = hgrn_prompt(qq, kk, vv, logf)
    else:
        ckv_p, kr_p, S0 = past
        k_pos = jnp.concatenate([jnp.arange(ckv_p.shape[1], dtype=jnp.int32), pos])
        o_lat = mla_attend(q_lat, q_rope, jnp.concatenate([ckv_p, ckv], 1),
                           jnp.concatenate([kr_p, krope], 1), pos, k_pos)
        S_fin, o_h = hgrn_chunk(S0.astype(jnp.float32), qq, kk, vv, logf)

    y_a = jnp.einsum('bshr,rhv->bshv', o_lat, lw['w_uv']).reshape(B, S, MLA_HEADS * MLA_D_V) @ lw['w_pa']
    o_h = rms_norm(o_h, lw['hg_norm_g']).reshape(B, S, HG_HEADS * HG_DV)
    y_b = (o_h * jax.nn.silu(hg.astype(jnp.float32))).astype(x.dtype) @ lw['w_pb']

    u = jax.nn.gelu(cu, approximate=False)
    v_cm = layer_norm(jax.nn.gelu(cv, approximate=False), lw['cm_ln_g'], lw['cm_ln_b'])
    y_c = chunk_mlp(u, v_cm, lw['cm_ws'], lw['cm_bs']) @ lw['w_pc']

    m = jax.nn.sigmoid(ga) * y_a + jax.nn.sigmoid(gb) * y_b + jax.nn.sigmoid(gc) * y_c
    x = layer_norm(ALPHA * x + m @ lw['w_o'], lw['ln1_g'], lw['ln1_b'])

    gate, up = jnp.split(x @ lw['w_up'], 2, axis=-1)
    x = layer_norm(ALPHA * x + (jax.nn.silu(gate) * up) @ lw['w_down'], lw['ln2_g'], lw['ln2_b'])
    return x, (ckv, krope, S_fin.astype(x.dtype), v_cm)


def setup_inputs(seed: int = 0) -> dict:
    key = jax.random.key(seed)
    ks = jax.random.split(key, 32)
    f32 = jnp.float32
    nrm = lambda k, shape, scale: jax.random.normal(k, shape, f32) * scale
    gain = lambda k, shape: 1.0 + 0.02 * jax.random.normal(k, shape, f32)
    bias = lambda k, shape: 0.02 * jax.random.normal(k, shape, f32)
    L = DEPTH
    return {
        "x_prompt": nrm(ks[0], (BATCH, SEQ, D_MODEL), 1.0),
        "x_sample": nrm(ks[1], (DEC_BATCH, DEC_SEQ, D_MODEL), 1.0),
        "cache_mla_ckv": nrm(ks[2], (L, DEC_BATCH, PAST_LEN, MLA_KV_RANK), 1.0),
        "cache_mla_krope": nrm(ks[3], (L, DEC_BATCH, PAST_LEN, MLA_D_ROPE), 1.0),
        "state_hgrn": nrm(ks[4], (L, DEC_BATCH, HG_HEADS, HG_DK, HG_DV), 0.5),
        "emb_ln_g": gain(ks[5], (D_MODEL,)),
        "emb_ln_b": bias(ks[6], (D_MODEL,)),
        "w_in": nrm(ks[7], (L, D_MODEL, D_IN), D_MODEL ** -0.5),
        "q_norm_g": gain(ks[8], (L, MLA_Q_RANK)),
        "w_uq": nrm(ks[9], (L, MLA_Q_RANK, MLA_HEADS * (MLA_D_NOPE + MLA_D_ROPE)), MLA_Q_RANK ** -0.5),
        "w_uk": nrm(ks[10], (L, MLA_KV_RANK, MLA_HEADS, MLA_D_NOPE), MLA_KV_RANK ** -0.5),
        "kv_norm_g": gain(ks[11], (L, MLA_KV_RANK)),
        "w_uv": nrm(ks[12], (L, MLA_KV_RANK, MLA_HEADS, MLA_D_V), MLA_KV_RANK ** -0.5),
        "hg_lb": nrm(ks[13], (L, HG_HEADS * HG_DK), 1.0),
        "hg_norm_g": gain(ks[14], (L, HG_HEADS, HG_DV)),
        "cm_ln_g": gain(ks[15], (L, CM_WIDTH)),
        "cm_ln_b": bias(ks[16], (L, CM_WIDTH)),
        "cm_ws": nrm(ks[17], (L, CM_GROUPS, CM_CHUNK, CM_CHUNK), CM_CHUNK ** -0.5),
        "cm_bs": gain(ks[18], (L, CM_GROUPS, CM_CHUNK)),
        "w_pa": nrm(ks[19], (L, MLA_HEADS * MLA_D_V, D_MODEL), (MLA_HEADS * MLA_D_V) ** -0.5 * BETA),
        "w_pb": nrm(ks[20], (L, HG_HEADS * HG_DV, D_MODEL), (HG_HEADS * HG_DV) ** -0.5 * BETA),
        "w_pc": nrm(ks[21], (L, CM_WIDTH, D_MODEL), CM_WIDTH ** -0.5 * BETA),
        "w_o": nrm(ks[22], (L, D_MODEL, D_MODEL), D_MODEL ** -0.5 * BETA),
        "ln1_g": gain(ks[23], (L, D_MODEL)),
        "ln1_b": bias(ks[24], (L, D_MODEL)),
        "w_up": nrm(ks[25], (L, D_MODEL, 2 * D_FF), D_MODEL ** -0.5),
        "w_down": nrm(ks[26], (L, D_FF, D_MODEL), D_FF ** -0.5 * BETA),
        "ln2_g": gain(ks[27], (L, D_MODEL)),
        "ln2_b": bias(ks[28], (L, D_MODEL)),
    }


def reference(x_prompt, x_sample, cache_mla_ckv, cache_mla_krope, state_hgrn, emb_ln_g, emb_ln_b,
              w_in, q_norm_g, w_uq, w_uk, kv_norm_g, w_uv, hg_lb, hg_norm_g, cm_ln_g, cm_ln_b,
              cm_ws, cm_bs, w_pa, w_pb, w_pc, w_o, ln1_g, ln1_b, w_up, w_down, ln2_g, ln2_b):
    sm = jax.nn.softmax(hg_lb.astype(jnp.float32), axis=0)
    lb_all = jnp.concatenate([jnp.zeros_like(sm[:1]), jnp.cumsum(sm[1:], axis=0)], axis=0)

    pos_p = jnp.arange(x_prompt.shape[1], dtype=jnp.int32)
    pos_s = cache_mla_ckv.shape[2] + jnp.arange(x_sample.shape[1], dtype=jnp.int32)

    xp = layer_norm(x_prompt, emb_ln_g, emb_ln_b)
    xs = layer_norm(x_sample, emb_ln_g, emb_ln_b)
    ckv_p_l, kr_p_l, s_p_l, ckv_s_l, kr_s_l, s_s_l, v_s_l = [], [], [], [], [], [], []
    for l in range(DEPTH):
        lw = dict(w_in=w_in[l], q_norm_g=q_norm_g[l], w_uq=w_uq[l], w_uk=w_uk[l], kv_norm_g=kv_norm_g[l],
                  w_uv=w_uv[l], hg_norm_g=hg_norm_g[l], cm_ln_g=cm_ln_g[l], cm_ln_b=cm_ln_b[l],
                  cm_ws=cm_ws[l], cm_bs=cm_bs[l], w_pa=w_pa[l], w_pb=w_pb[l], w_pc=w_pc[l], w_o=w_o[l],
                  ln1_g=ln1_g[l], ln1_b=ln1_b[l], w_up=w_up[l], w_down=w_down[l],
                  ln2_g=ln2_g[l], ln2_b=ln2_b[l])
        xp, (ckv_p, kr_p, s_p, _) = trunk_layer(xp, pos_p, lw, lb_all[l], None)
        xs, (ckv_s, kr_s, s_s, v_s) = trunk_layer(
            xs, pos_s, lw, lb_all[l], (cache_mla_ckv[l], cache_mla_krope[l], state_hgrn[l]))
        ckv_p_l.append(ckv_p); kr_p_l.append(kr_p); s_p_l.append(s_p)
        ckv_s_l.append(ckv_s); kr_s_l.append(kr_s); s_s_l.append(s_s); v_s_l.append(v_s)

    return (xp, xs, jnp.stack(ckv_p_l), jnp.stack(kr_p_l), jnp.stack(s_p_l),
            jnp.stack(ckv_s_l), jnp.stack(kr_s_l), jnp.stack(s_s_l), jnp.stack(v_s_l))
```

```python
import functools

import numpy as np
import jax
import jax.numpy as jnp
from jax import lax
from jax.experimental import pallas as pl
from jax.experimental.pallas import tpu as pltpu

F32 = jnp.float32
BF16 = jnp.bfloat16

CHUNK = 64
MLA_HEADS = 8
MLA_D_NOPE = 64
MLA_D_ROPE = 32
MLA_D_V = 64
MLA_RANK = 256
MLA_SCALE = (MLA_D_NOPE + MLA_D_ROPE) ** -0.5
ROPE_THETA = 10000.0
MASK_VALUE = -1e30
HG_HEADS = 4
HG_DK = 128
HG_DV = 128
HG_W = HG_HEADS * HG_DK
LB_TINY = 1e-30
CM_CHUNK = 128
CM_GROUPS = 4
CM_WIDTH = 512
CM_GROUP_DIM = CM_WIDTH // CM_GROUPS
EPS = 1e-5

LANES = 128
VMEM_LIMIT_BYTES = 56 * 1024 * 1024
QK_WIDTH = MLA_RANK + LANES
KV_BLOCK = 256

_O_CQ, _O_CKV, _O_HQ, _O_HF, _O_HI, _O_HG, _O_CU, _O_CV, _O_GA, _O_GB, _O_GC, _O_KR, _O_END = (
    0, 256, 512, 1024, 1536, 2048, 2560, 3072, 3584, 4608, 5632, 6656, 6912)


def _const_spec(shape):
    nd = len(shape)
    return pl.BlockSpec(shape, lambda *_: (0,) * nd, pipeline_mode=pl.Buffered(1))


def _params(*sem):
    return pltpu.CompilerParams(dimension_semantics=sem, vmem_limit_bytes=VMEM_LIMIT_BYTES)


def _layer_norm(x, g, b):
    mu = jnp.mean(x, -1, keepdims=True)
    xc = x - mu
    var = jnp.mean(xc * xc, -1, keepdims=True)
    return xc * lax.rsqrt(var + EPS) * g + b


def _rms_norm(x, g):
    return x * lax.rsqrt(jnp.mean(x * x, -1, keepdims=True) + EPS) * g


def _gelu(x):
    return 0.5 * x * (1.0 + lax.erf(x * np.float32(1.0 / np.sqrt(2.0))))


def _sigmoid(x):
    return 1.0 / (1.0 + jnp.exp(-x))


def _silu(x):
    return x * _sigmoid(x)


def _log_sigmoid(x):
    return jnp.minimum(x, 0.0) - jnp.log(1.0 + jnp.exp(-jnp.abs(x)))


def _ln_kernel(x_ref, g_ref, b_ref, o_ref):
    o_ref[...] = _layer_norm(x_ref[...], g_ref[...], b_ref[...])


def _input_ln(x, g, b, tm):
    m, d = x.shape
    return pl.pallas_call(
        _ln_kernel, grid=(m // tm,),
        in_specs=[pl.BlockSpec((tm, d), lambda i: (i, 0)), _const_spec((1, d)), _const_spec((1, d))],
        out_specs=pl.BlockSpec((tm, d), lambda i: (i, 0)),
        out_shape=jax.ShapeDtypeStruct((m, d), F32),
        compiler_params=_params("parallel"), name="input_ln",
    )(x, g, b)


def _proj_kernel(x_ref, win_ref, wuq_ref, wuk_ref, qg_ref, kvg_ref, rope_ref, lbp_ref, cmg_ref, cmb_ref,
                 q_ref, kv_ref, ckv_ref, kr_ref, hq_ref, hk_ref, hlf_ref, hv_ref, hgt_ref,
                 u_ref, vcm_ref, sga_ref, sgb_ref, sgc_ref):
    xb = x_ref[...].astype(BF16)

    def proj(lo, hi):
        return jnp.dot(xb, win_ref[:, lo:hi], preferred_element_type=F32)

    cos_t = rope_ref[:, :LANES]
    sin_t = rope_ref[:, LANES:]

    cqn = _rms_norm(proj(_O_CQ, _O_CKV), qg_ref[...]).astype(BF16)
    q3 = jnp.dot(cqn, wuq_ref[...], preferred_element_type=F32)
    hw = MLA_HEADS * LANES
    for h in range(MLA_HEADS):
        nope = q3[:, h * LANES:(h + 1) * LANES].astype(BF16)
        lat = jnp.dot(nope, wuk_ref[h], preferred_element_type=F32) * MLA_SCALE
        rot = (q3[:, hw + h * LANES:hw + (h + 1) * LANES] * cos_t
               + q3[:, 2 * hw + h * LANES:2 * hw + (h + 1) * LANES] * sin_t) * MLA_SCALE
        q_ref[h, :, :MLA_RANK] = lat.astype(BF16)
        q_ref[h, :, MLA_RANK:] = rot.astype(BF16)

    ckv = _rms_norm(proj(_O_CKV, _O_HQ), kvg_ref[...])
    ckv_ref[...] = ckv
    kv_ref[:, :MLA_RANK] = ckv.astype(BF16)
    zk = proj(_O_KR, _O_END)
    krot = zk[:, :LANES] * cos_t + zk[:, LANES:] * sin_t
    kr_ref[...] = krot[:, :MLA_D_ROPE]
    kv_ref[:, MLA_RANK:] = krot.astype(BF16)

    hq_ref[...] = proj(_O_HQ, _O_HF)
    zf = proj(_O_HF, _O_HI)
    log_lb = lbp_ref[0:1, :]
    log_1m = lbp_ref[1:2, :]
    one_m = lbp_ref[2:3, :]
    c = log_1m + _log_sigmoid(zf)
    hlf_ref[...] = jnp.maximum(log_lb, c) + jnp.log(1.0 + jnp.exp(-jnp.abs(log_lb - c)))
    hk_ref[...] = one_m * _sigmoid(-zf)
    hv_ref[...] = proj(_O_HI, _O_HG).astype(BF16)
    hgt_ref[...] = _silu(proj(_O_HG, _O_CU)).astype(BF16)

    u_ref[...] = _gelu(proj(_O_CU, _O_CV)).astype(BF16)
    vcm_ref[...] = _layer_norm(_gelu(proj(_O_CV, _O_GA)), cmg_ref[...], cmb_ref[...]).astype(vcm_ref.dtype)

    sga_ref[...] = _sigmoid(proj(_O_GA, _O_GB)).astype(BF16)
    sgb_ref[...] = _sigmoid(proj(_O_GB, _O_GC)).astype(BF16)
    sgc_ref[...] = _sigmoid(proj(_O_GC, _O_KR)).astype(BF16)


def _proj(x, lw, rope_tab, tm, vcm_dtype):
    m, d = x.shape
    row = lambda w: pl.BlockSpec((tm, w), lambda i: (i, 0))
    out_shape = [
        jax.ShapeDtypeStruct((MLA_HEADS, m, QK_WIDTH), BF16),
        jax.ShapeDtypeStruct((m, QK_WIDTH), BF16),
        jax.ShapeDtypeStruct((m, MLA_RANK), F32),
        jax.ShapeDtypeStruct((m, MLA_D_ROPE), F32),
        jax.ShapeDtypeStruct((m, HG_W), F32),
        jax.ShapeDtypeStruct((m, HG_W), F32),
        jax.ShapeDtypeStruct((m, HG_W), F32),
        jax.ShapeDtypeStruct((m, HG_W), BF16),
        jax.ShapeDtypeStruct((m, HG_W), BF16),
        jax.ShapeDtypeStruct((m, CM_WIDTH), BF16),
        jax.ShapeDtypeStruct((m, CM_WIDTH), vcm_dtype),
        jax.ShapeDtypeStruct((m, d), BF16),
        jax.ShapeDtypeStruct((m, d), BF16),
        jax.ShapeDtypeStruct((m, d), BF16),
    ]
    out_specs = [
        pl.BlockSpec((MLA_HEADS, tm, QK_WIDTH), lambda i: (0, i, 0)),
        row(QK_WIDTH), row(MLA_RANK), row(MLA_D_ROPE),
        row(HG_W), row(HG_W), row(HG_W), row(HG_W), row(HG_W),
        row(CM_WIDTH), row(CM_WIDTH), row(d), row(d), row(d),
    ]
    in_specs = [
        row(d),
        _const_spec(lw["w_in"].shape), _const_spec(lw["w_uq"].shape), _const_spec(lw["w_uk"].shape),
        _const_spec((1, MLA_RANK)), _const_spec((1, MLA_RANK)),
        row(2 * LANES),
        _const_spec((3, HG_W)), _const_spec((1, CM_WIDTH)), _const_spec((1, CM_WIDTH)),
    ]
    return pl.pallas_call(
        _proj_kernel, grid=(m // tm,), in_specs=in_specs, out_specs=out_specs, out_shape=out_shape,
        compiler_params=_params("parallel"), name="proj",
    )(x, lw["w_in"], lw["w_uq"], lw["w_uk"], lw["q_norm_g"], lw["kv_norm_g"], rope_tab,
      lw["lbp"], lw["cm_ln_g"], lw["cm_ln_b"])


def _attn_prompt_kernel(q_ref, k_ref, o_ref, m_ref, l_ref, acc_ref):
    c = pl.program_id(1)
    rows = MLA_HEADS * CHUNK
    q = q_ref[...].reshape(rows, QK_WIDTH)
    kv_len = (c + 1) * CHUNK
    nblk = (kv_len + KV_BLOCK - 1) // KV_BLOCK
    m_ref[...] = jnp.full(m_ref.shape, MASK_VALUE, F32)
    l_ref[...] = jnp.zeros(l_ref.shape, F32)
    acc_ref[...] = jnp.zeros(acc_ref.shape, F32)

    def body(j, carry):
        start = pl.multiple_of(j * KV_BLOCK, KV_BLOCK)
        k = k_ref[pl.ds(start, KV_BLOCK), :]
        s = lax.dot_general(q, k, (((1,), (1,)), ((), ())), preferred_element_type=F32)
        kidx = start + lax.broadcasted_iota(jnp.int32, s.shape, 1)
        s = jnp.where(kidx < kv_len, s, MASK_VALUE)
        m_prev = m_ref[...]
        m_new = jnp.maximum(m_prev, jnp.max(s, -1, keepdims=True))
        alpha = jnp.exp(m_prev - m_new)
        p = jnp.exp(s - m_new)
        l_ref[...] = alpha * l_ref[...] + jnp.sum(p, -1, keepdims=True)
        acc_ref[...] = alpha * acc_ref[...] + jnp.dot(p.astype(BF16), k[:, :MLA_RANK],
                                                      preferred_element_type=F32)
        m_ref[...] = m_new
        return carry

    lax.fori_loop(0, nblk, body, 0)
    o = acc_ref[...] * (1.0 / l_ref[...])
    for h in range(MLA_HEADS):
        o_ref[:, h * MLA_RANK:(h + 1) * MLA_RANK] = o[h * CHUNK:(h + 1) * CHUNK, :].astype(BF16)


def _attn_prompt(q, k, batch, seq):
    m = batch * seq
    nqc = seq // CHUNK
    rows = MLA_HEADS * CHUNK
    return pl.pallas_call(
        _attn_prompt_kernel, grid=(batch, nqc),
        in_specs=[pl.BlockSpec((MLA_HEADS, CHUNK, QK_WIDTH), lambda b, c: (0, b * nqc + c, 0)),
                  pl.BlockSpec((seq, QK_WIDTH), lambda b, c: (b, 0))],
        out_specs=pl.BlockSpec((CHUNK, MLA_HEADS * MLA_RANK), lambda b, c: (b * nqc + c, 0)),
        out_shape=jax.ShapeDtypeStruct((m, MLA_HEADS * MLA_RANK), BF16),
        scratch_shapes=[pltpu.VMEM((rows, 1), F32), pltpu.VMEM((rows, 1), F32),
                        pltpu.VMEM((rows, MLA_RANK), F32)],
        compiler_params=_params("parallel", "arbitrary"), name="attn_prompt",
    )(q, k)


def _attn_sample_kernel(past_len, q_ref, k_ref, cckv_ref, ckr_ref, o_ref):
    nq = q_ref.shape[1]
    rows = MLA_HEADS * nq
    q = q_ref[...].reshape(rows, QK_WIDTH)
    q_lat = q[:, :MLA_RANK]
    q_rot = q[:, MLA_RANK:MLA_RANK + MLA_D_ROPE]
    nt = (((1,), (1,)), ((), ()))
    kc = cckv_ref[...].astype(BF16)
    s1 = (lax.dot_general(q_lat, kc, nt, preferred_element_type=F32)
          + lax.dot_general(q_rot, ckr_ref[...].astype(BF16), nt, preferred_element_type=F32))
    kn = k_ref[...]
    s2 = lax.dot_general(q, kn, nt, preferred_element_type=F32)
    q_pos = past_len + lax.broadcasted_iota(jnp.int32, s2.shape, 0) % nq
    k_pos = past_len + lax.broadcasted_iota(jnp.int32, s2.shape, 1)
    s2 = jnp.where(k_pos // CHUNK <= q_pos // CHUNK, s2, MASK_VALUE)
    mx = jnp.maximum(jnp.max(s1, -1, keepdims=True), jnp.max(s2, -1, keepdims=True))
    p1 = jnp.exp(s1 - mx)
    p2 = jnp.exp(s2 - mx)
    l = jnp.sum(p1, -1, keepdims=True) + jnp.sum(p2, -1, keepdims=True)
    o = (jnp.dot(p1.astype(BF16), kc, preferred_element_type=F32)
         + jnp.dot(p2.astype(BF16), kn[:, :MLA_RANK], preferred_element_type=F32)) * (1.0 / l)
    for h in range(MLA_HEADS):
        o_ref[:, h * MLA_RANK:(h + 1) * MLA_RANK] = o[h * nq:(h + 1) * nq, :].astype(BF16)


def _attn_sample(q, k, cache_ckv, cache_kr, batch, nq):
    past_len = cache_ckv.shape[1]
    return pl.pallas_call(
        functools.partial(_attn_sample_kernel, past_len), grid=(batch,),
        in_specs=[pl.BlockSpec((MLA_HEADS, nq, QK_WIDTH), lambda b: (0, b, 0)),
                  pl.BlockSpec((nq, QK_WIDTH), lambda b: (b, 0)),
                  pl.BlockSpec((None, past_len, MLA_RANK), lambda b: (b, 0, 0)),
                  pl.BlockSpec((None, past_len, MLA_D_ROPE), lambda b: (b, 0, 0))],
        out_specs=pl.BlockSpec((nq, MLA_HEADS * MLA_RANK), lambda b: (b, 0)),
        out_shape=jax.ShapeDtypeStruct((batch * nq, MLA_HEADS * MLA_RANK), BF16),
        compiler_params=_params("parallel"), name="attn_sample",
    )(q, k, cache_ckv, cache_kr)


def _hgrn_levels(lc):
    spans = []
    sp = lc // 2
    while sp >= 1:
        spans.append(sp)
        sp //= 2
    return spans


def _hgrn_consts(lc):
    spans = _hgrn_levels(lc)
    tri = np.tril(np.ones((lc, lc), np.float32))
    t = np.arange(lc)
    mats, masks = [tri], []
    for sp in spans:
        blk = t // (2 * sp)
        ref_row = blk * 2 * sp + sp - 1
        mats.append(tri[ref_row])
        right = (t % (2 * sp)) >= sp
        masks.append(((blk[:, None] == blk[None, :]) & right[:, None] & ~right[None, :]).astype(np.float32))
    return np.concatenate(mats, 0), np.stack(masks, 0)


def _hgrn_kernel(has_init, q_ref, k_ref, lf_ref, v_ref, gt_ref, gn_ref, cmat_ref, mask_ref, s0_ref,
                 o_ref, sout_ref, st_ref):
    c = pl.program_id(1)
    lc = q_ref.shape[0]
    nlev = mask_ref.shape[0]
    nt = (((1,), (1,)), ((), ()))
    tn = (((0,), (0,)), ((), ()))

    @pl.when(c == 0)
    def _():
        for h in range(HG_HEADS):
            if has_init:
                st_ref[h] = s0_ref[h].T
            else:
                st_ref[h] = jnp.zeros((HG_DV, HG_DK), F32)

    br = jnp.dot(cmat_ref[...], lf_ref[...], precision=lax.Precision.HIGHEST, preferred_element_type=F32)
    eye = (lax.broadcasted_iota(jnp.int32, (lc, lc), 0) == lax.broadcasted_iota(jnp.int32, (lc, lc), 1))
    for h in range(HG_HEADS):
        sl = slice(h * HG_DK, (h + 1) * HG_DK)
        bh = br[:lc, sl]
        qh = q_ref[:, sl]
        kh = k_ref[:, sl]
        vh = v_ref[:, sl]
        a = jnp.where(eye, jnp.sum(qh * kh, -1, keepdims=True), 0.0)
        for lv in range(nlev):
            ref = br[(1 + lv) * lc:(2 + lv) * lc, sl]
            e = jnp.exp(-jnp.abs(bh - ref))
            a = a + mask_ref[lv] * lax.dot_general((qh * e).astype(BF16), (kh * e).astype(BF16), nt,
                                                   preferred_element_type=F32)
        st = st_ref[h]
        o = (jnp.dot(a.astype(BF16), vh, preferred_element_type=F32)
             + lax.dot_general((qh * jnp.exp(bh)).astype(BF16), st.astype(BF16), nt,
                               preferred_element_type=F32))
        b_last = bh[lc - 1:lc, :]
        kdec = (kh * jnp.exp(b_last - bh)).astype(BF16)
        st_ref[h] = st * jnp.exp(b_last) + lax.dot_general(vh, kdec, tn, preferred_element_type=F32)
        on = _rms_norm(o, gn_ref[:, sl])
        o_ref[:, sl] = (on * gt_ref[:, sl].astype(F32)).astype(BF16)

    @pl.when(c == pl.num_programs(1) - 1)
    def _():
        for h in range(HG_HEADS):
            sout_ref[h] = st_ref[h].T


def _hgrn(hq, hk, hlf, hv, hgt, gnorm, s0, batch, seq, lc):
    m = batch * seq
    nch = seq // lc
    cmat, masks = _hgrn_consts(lc)
    has_init = s0 is not None
    if s0 is None:
        s0 = jnp.zeros((batch, HG_HEADS, HG_DK, HG_DV), F32)
    row = pl.BlockSpec((lc, HG_W), lambda b, c: (b * nch + c, 0))
    st_spec = pl.BlockSpec((None, HG_HEADS, HG_DK, HG_DV), lambda b, c: (b, 0, 0, 0))
    return pl.pallas_call(
        functools.partial(_hgrn_kernel, has_init), grid=(batch, nch),
        in_specs=[row, row, row, row, row, _const_spec((1, HG_W)),
                  _const_spec(cmat.shape), _const_spec(masks.shape), st_spec],
        out_specs=[row, st_spec],
        out_shape=[jax.ShapeDtypeStruct((m, HG_W), BF16),
                   jax.ShapeDtypeStruct((batch, HG_HEADS, HG_DK, HG_DV), F32)],
        scratch_shapes=[pltpu.VMEM((HG_HEADS, HG_DV, HG_DK), F32)],
        compiler_params=_params("parallel", "arbitrary"), name="hgrn",
    )(hq, hk, hlf, hv, hgt, gnorm, jnp.asarray(cmat), jnp.asarray(masks), s0)


def _merge_kernel(alpha, cml, x_ref, ol_ref, hg_ref, u_ref, vcm_ref, sga_ref, sgb_ref, sgc_ref,
                  wuv_ref, wpa_ref, wpb_ref, wpc_ref, wo_ref, ws_ref, bs_ref, g_ref, b_ref,
                  o_ref, cin_ref):
    tm = x_ref.shape[0]
    tril = (lax.broadcasted_iota(jnp.int32, (cml, cml), 0) >= lax.broadcasted_iota(jnp.int32, (cml, cml), 1))
    for g in range(CM_GROUPS):
        sl = slice(g * CM_GROUP_DIM, (g + 1) * CM_GROUP_DIM)
        wg = jnp.where(tril, ws_ref[g], 0.0).astype(BF16)
        for r in range(tm // cml):
            rs = slice(r * cml, (r + 1) * cml)
            s = jnp.dot(wg, vcm_ref[rs, sl].astype(BF16), preferred_element_type=F32) + bs_ref[:, sl]
            cin_ref[rs, sl] = (u_ref[rs, sl].astype(F32) * s).astype(BF16)
    a_in = jnp.dot(ol_ref[...], wuv_ref[...], preferred_element_type=F32).astype(BF16)
    y_a = jnp.dot(a_in, wpa_ref[...], preferred_element_type=F32)
    y_b = jnp.dot(hg_ref[...], wpb_ref[...], preferred_element_type=F32)
    y_c = jnp.dot(cin_ref[...], wpc_ref[...], preferred_element_type=F32)
    mrg = (sga_ref[...].astype(F32) * y_a + sgb_ref[...].astype(F32) * y_b
           + sgc_ref[...].astype(F32) * y_c).astype(BF16)
    y = alpha * x_ref[...] + jnp.dot(mrg, wo_ref[...], preferred_element_type=F32)
    o_ref[...] = _layer_norm(y, g_ref[...], b_ref[...])


def _merge(x, o_lat, hg_o, u, vcm, sga, sgb, sgc, lw, alpha, cml, tm):
    m, d = x.shape
    row = lambda w: pl.BlockSpec((tm, w), lambda i: (i, 0))
    ws = lw["cm_ws"][:, :cml, :cml]
    bs = lw["cm_bs_full"][:cml]
    in_specs = [row(d), row(MLA_HEADS * MLA_RANK), row(HG_W), row(CM_WIDTH), row(CM_WIDTH), row(d), row(d), row(d),
                _const_spec(lw["w_uv"].shape), _const_spec(lw["w_pa"].shape), _const_spec(lw["w_pb"].shape),
                _const_spec(lw["w_pc"].shape), _const_spec(lw["w_o"].shape),
                _const_spec(ws.shape), _const_spec(bs.shape), _const_spec((1, d)), _const_spec((1, d))]
    return pl.pallas_call(
        functools.partial(_merge_kernel, alpha, cml), grid=(m // tm,), in_specs=in_specs,
        out_specs=row(d), out_shape=jax.ShapeDtypeStruct((m, d), F32),
        scratch_shapes=[pltpu.VMEM((tm, CM_WIDTH), BF16)],
        compiler_params=_params("parallel"), name="merge",
    )(x, o_lat, hg_o, u, vcm, sga, sgb, sgc, lw["w_uv"], lw["w_pa"], lw["w_pb"], lw["w_pc"], lw["w_o"],
      ws, bs, lw["ln1_g"], lw["ln1_b"])


def _ffn_kernel(alpha, nsplit, x_ref, wup_ref, wdn_ref, g_ref, b_ref, o_ref):
    x = x_ref[...]
    xb = x.astype(BF16)
    dff = wdn_ref.shape[0]
    cw = dff // nsplit
    y = alpha * x
    for j in range(nsplit):
        gate = jnp.dot(xb, wup_ref[:, j * cw:(j + 1) * cw], preferred_element_type=F32)
        up = jnp.dot(xb, wup_ref[:, dff + j * cw:dff + (j + 1) * cw], preferred_element_type=F32)
        act = (_silu(gate) * up).astype(BF16)
        y = y + jnp.dot(act, wdn_ref[j * cw:(j + 1) * cw, :], preferred_element_type=F32)
    o_ref[...] = _layer_norm(y, g_ref[...], b_ref[...])


def _ffn(x, lw, alpha, tm):
    m, d = x.shape
    row = pl.BlockSpec((tm, d), lambda i: (i, 0))
    return pl.pallas_call(
        functools.partial(_ffn_kernel, alpha, 2), grid=(m // tm,),
        in_specs=[row, _const_spec(lw["w_up"].shape), _const_spec(lw["w_down"].shape),
                  _const_spec((1, d)), _const_spec((1, d))],
        out_specs=row, out_shape=jax.ShapeDtypeStruct((m, d), F32),
        compiler_params=_params("parallel"), name="ffn",
    )(x, lw["w_up"], lw["w_down"], lw["ln2_g"], lw["ln2_b"])


def _rope_table(pos):
    half = MLA_D_ROPE // 2
    inv = 1.0 / (ROPE_THETA ** (jnp.arange(half, dtype=F32) / half))
    ang = pos.astype(F32)[:, None] * inv[None]
    cos, sin = jnp.cos(ang), jnp.sin(ang)
    z = jnp.zeros((pos.shape[0], LANES - MLA_D_ROPE), F32)
    return jnp.concatenate([cos, cos, z, -sin, sin, z], -1)


def _prep_weights(w_in, q_norm_g, w_uq, w_uk, kv_norm_g, w_uv, hg_lb, hg_norm_g, cm_ln_g, cm_ln_b,
                  cm_ws, cm_bs, w_pa, w_pb, w_pc, w_o, ln1_g, ln1_b, w_up, w_down, ln2_g, ln2_b):
    depth, d, _ = w_in.shape
    half = MLA_D_ROPE // 2
    cols = np.cumsum([0, MLA_RANK, MLA_RANK, MLA_D_ROPE, HG_W, HG_W, HG_W, HG_W, CM_WIDTH, CM_WIDTH, d, d, d])
    seg = lambda i: w_in[:, :, cols[i]:cols[i + 1]]
    kr = seg(2)
    zk = jnp.zeros((depth, d, LANES - MLA_D_ROPE), F32)
    w_in_r = jnp.concatenate(
        [seg(0), seg(1)] + [seg(i) for i in range(3, 12)]
        + [kr, zk, kr[..., half:], kr[..., :half], zk], -1).astype(BF16)

    dq = MLA_D_NOPE + MLA_D_ROPE
    uq = w_uq.reshape(depth, MLA_RANK, MLA_HEADS, dq)
    nope = uq[..., :MLA_D_NOPE]
    r1, r2 = uq[..., MLA_D_NOPE:MLA_D_NOPE + half], uq[..., MLA_D_NOPE + half:]
    zn = jnp.zeros((depth, MLA_RANK, MLA_HEADS, LANES - MLA_D_NOPE), F32)
    zr = jnp.zeros((depth, MLA_RANK, MLA_HEADS, LANES - MLA_D_ROPE), F32)
    flat = lambda t: t.reshape(depth, MLA_RANK, MLA_HEADS * LANES)
    w_uq_r = jnp.concatenate([flat(jnp.concatenate([nope, zn], -1)),
                              flat(jnp.concatenate([r1, r2, zr], -1)),
                              flat(jnp.concatenate([r2, r1, zr], -1))], -1).astype(BF16)
    ukt = jnp.transpose(w_uk, (0, 2, 3, 1))
    w_uk_r = jnp.concatenate([ukt, jnp.zeros((depth, MLA_HEADS, LANES - MLA_D_NOPE, MLA_RANK), F32)], 2).astype(BF16)
    uvt = jnp.transpose(w_uv, (0, 2, 1, 3))
    eye = jnp.eye(MLA_HEADS, dtype=F32)
    w_uv_r = (uvt[:, :, :, None, :] * eye[None, :, None, :, None]).reshape(
        depth, MLA_HEADS * MLA_RANK, MLA_HEADS * MLA_D_V).astype(BF16)

    sm = jax.nn.softmax(hg_lb.astype(F32), axis=0)
    lb = jnp.concatenate([jnp.zeros_like(sm[:1]), jnp.cumsum(sm[1:], axis=0)], axis=0)
    lbp = jnp.stack([jnp.log(lb + LB_TINY), jnp.log1p(-lb), 1.0 - lb], 1)

    bs_full = jnp.repeat(jnp.transpose(cm_bs, (0, 2, 1)), CM_GROUP_DIM, axis=-1)

    r3 = lambda t: t.reshape(depth, 1, -1)
    return dict(w_in=w_in_r, w_uq=w_uq_r, w_uk=w_uk_r, w_uv=w_uv_r, lbp=lbp,
                q_norm_g=r3(q_norm_g), kv_norm_g=r3(kv_norm_g), hg_norm_g=r3(hg_norm_g),
                cm_ln_g=r3(cm_ln_g), cm_ln_b=r3(cm_ln_b), cm_ws=cm_ws, cm_bs_full=bs_full,
                w_pa=w_pa.astype(BF16), w_pb=w_pb.astype(BF16), w_pc=w_pc.astype(BF16), w_o=w_o.astype(BF16),
                ln1_g=r3(ln1_g), ln1_b=r3(ln1_b), w_up=w_up.astype(BF16), w_down=w_down.astype(BF16),
                ln2_g=r3(ln2_g), ln2_b=r3(ln2_b))


def _row_tile(m, want):
    tm = min(m, want)
    assert m % tm == 0, (m, tm)
    return tm


def _layer(x, lw, rope_tab, alpha, batch, seq, past, tm):
    prompt = past is None
    outs = _proj(x, lw, rope_tab, tm, BF16 if prompt else F32)
    q, k, ckv, kr, hq, hk, hlf, hv, hgt, u, vcm, sga, sgb, sgc = outs
    if prompt:
        o_lat = _attn_prompt(q, k, batch, seq)
        hg_o, s_fin = _hgrn(hq, hk, hlf, hv, hgt, lw["hg_norm_g"], None, batch, seq, CHUNK)
    else:
        o_lat = _attn_sample(q, k, past[0], past[1], batch, seq)
        hg_o, s_fin = _hgrn(hq, hk, hlf, hv, hgt, lw["hg_norm_g"], past[2], batch, seq, seq)
    cml = min(seq, CM_CHUNK)
    x = _merge(x, o_lat, hg_o, u, vcm, sga, sgb, sgc, lw, alpha, cml, tm)
    x = _ffn(x, lw, alpha, tm)
    return x, (ckv, kr, s_fin, vcm)


def kernel(x_prompt, x_sample, cache_mla_ckv, cache_mla_krope, state_hgrn, emb_ln_g, emb_ln_b, w_in, q_norm_g, w_uq, w_uk, kv_norm_g, w_uv, hg_lb, hg_norm_g, cm_ln_g, cm_ln_b, cm_ws, cm_bs, w_pa, w_pb, w_pc, w_o, ln1_g, ln1_b, w_up, w_down, ln2_g, ln2_b):
    bp, sp, d = x_prompt.shape
    bs, ss, _ = x_sample.shape
    depth = w_in.shape[0]
    past_len = cache_mla_ckv.shape[2]
    assert sp % CM_CHUNK == 0 and sp % KV_BLOCK == 0 and ss <= CHUNK and ss % 16 == 0
    alpha = float((2 * depth) ** 0.25)

    wts = _prep_weights(w_in, q_norm_g, w_uq, w_uk, kv_norm_g, w_uv, hg_lb, hg_norm_g, cm_ln_g, cm_ln_b,
                        cm_ws, cm_bs, w_pa, w_pb, w_pc, w_o, ln1_g, ln1_b, w_up, w_down, ln2_g, ln2_b)
    rope_p = jnp.tile(_rope_table(jnp.arange(sp, dtype=jnp.int32)), (bp, 1))
    rope_s = jnp.tile(_rope_table(past_len + jnp.arange(ss, dtype=jnp.int32)), (bs, 1))

    tm_p = _row_tile(bp * sp, 256)
    tm_s = _row_tile(bs * ss, 256)
    eg, eb = emb_ln_g.reshape(1, d), emb_ln_b.reshape(1, d)
    xp = _input_ln(x_prompt.reshape(bp * sp, d), eg, eb, tm_p)
    xs = _input_ln(x_sample.reshape(bs * ss, d), eg, eb, tm_s)

    acc = [[] for _ in range(7)]
    for l in range(depth):
        lw = {name: v[l] for name, v in wts.items()}
        xp, (ckv_p, kr_p, s_p, _) = _layer(xp, lw, rope_p, alpha, bp, sp, None, tm_p)
        xs, (ckv_s, kr_s, s_s, v_s) = _layer(
            xs, lw, rope_s, alpha, bs, ss, (cache_mla_ckv[l], cache_mla_krope[l], state_hgrn[l]), tm_s)
        for lst, val in zip(acc, (ckv_p.reshape(bp, sp, -1), kr_p.reshape(bp, sp, -1), s_p,
                                  ckv_s.reshape(bs, ss, -1), kr_s.reshape(bs, ss, -1), s_s,
                                  v_s.reshape(bs, ss, -1))):
            lst.append(val)
    return (xp.reshape(bp, sp, d), xs.reshape(bs, ss, d)) + tuple(jnp.stack(a) for a in acc)
```

```python
import functools

import numpy as np
import jax
import jax.numpy as jnp
from jax import lax
from jax.experimental import pallas as pl
from jax.experimental.pallas import tpu as pltpu

F32 = jnp.float32
BF16 = jnp.bfloat16

CHUNK = 64
MLA_HEADS = 8
MLA_D_NOPE = 64
MLA_D_ROPE = 32
MLA_D_V = 64
MLA_RANK = 256
MLA_SCALE = (MLA_D_NOPE + MLA_D_ROPE) ** -0.5
LOG2E = float(np.log2(np.e))
Q_SCALE = MLA_SCALE * LOG2E
ROPE_THETA = 10000.0
MASK_VALUE = -1e30
HG_HEADS = 4
HG_DK = 128
HG_DV = 128
HG_W = HG_HEADS * HG_DK
LB_TINY = 1e-30
CM_CHUNK = 128
CM_GROUPS = 4
CM_WIDTH = 512
CM_GROUP_DIM = CM_WIDTH // CM_GROUPS
EPS = 1e-5

LANES = 128
VMEM_LIMIT_BYTES = 56 * 1024 * 1024
QK_WIDTH = MLA_RANK + LANES
KV_BLOCK = 256
Q_BLOCK = 128

_O_CQ, _O_CKV, _O_HQ, _O_HF, _O_HI, _O_HG, _O_CU, _O_CV, _O_GA, _O_GB, _O_GC, _O_KR, _O_END = (
    0, 256, 512, 1024, 1536, 2048, 2560, 3072, 3584, 4608, 5632, 6656, 6912)


def _const_spec(shape):
    nd = len(shape)
    return pl.BlockSpec(shape, lambda *_: (0,) * nd, pipeline_mode=pl.Buffered(1))


class _LayerWeights:
    def __init__(self, stacked, layer):
        self.stacked = stacked
        self.layer = layer

    def __getitem__(self, name):
        return self.stacked[name]

    def spec(self, name):
        shape = self.stacked[name].shape[1:]
        layer, nd = self.layer, len(shape)
        return pl.BlockSpec((None,) + shape, lambda *_: (layer,) + (0,) * nd, pipeline_mode=pl.Buffered(1))


def _params(*sem):
    return pltpu.CompilerParams(dimension_semantics=sem, vmem_limit_bytes=VMEM_LIMIT_BYTES)


def _layer_norm(x, g, b):
    mu = jnp.mean(x, -1, keepdims=True)
    xc = x - mu
    var = jnp.mean(xc * xc, -1, keepdims=True)
    return xc * lax.rsqrt(var + EPS) * g + b


def _rms_norm(x, g):
    return x * lax.rsqrt(jnp.mean(x * x, -1, keepdims=True) + EPS) * g


def _gelu(x):
    return 0.5 * x * (1.0 + lax.erf(x * np.float32(1.0 / np.sqrt(2.0))))


def _sigmoid(x):
    return 1.0 / (1.0 + jnp.exp(-x))


def _silu(x):
    return x * _sigmoid(x)


def _log_sigmoid(x):
    return jnp.minimum(x, 0.0) - jnp.log(1.0 + jnp.exp(-jnp.abs(x)))


def _ln_kernel(x_ref, g_ref, b_ref, o_ref):
    o_ref[...] = _layer_norm(x_ref[...], g_ref[...], b_ref[...])


def _input_ln(x, g, b, tm):
    m, d = x.shape
    return pl.pallas_call(
        _ln_kernel, grid=(m // tm,),
        in_specs=[pl.BlockSpec((tm, d), lambda i: (i, 0)), _const_spec((1, d)), _const_spec((1, d))],
        out_specs=pl.BlockSpec((tm, d), lambda i: (i, 0)),
        out_shape=jax.ShapeDtypeStruct((m, d), F32),
        compiler_params=_params("parallel"), name="input_ln",
    )(x, g, b)


def _proj_kernel(x_ref, win_ref, wuq_ref, wuk_ref, qg_ref, kvg_ref, rope_ref, lbp_ref, cmg_ref, cmb_ref,
                 q_ref, kv_ref, ckv_ref, kr_ref, hq_ref, hk_ref, hlf_ref, hv_ref, hgt_ref,
                 u_ref, vcm_ref, sga_ref, sgb_ref, sgc_ref):
    xb = x_ref[...].astype(BF16)

    def proj(lo, hi):
        return jnp.dot(xb, win_ref[:, lo:hi], preferred_element_type=F32)

    cos_t = rope_ref[:, :LANES]
    sin_t = rope_ref[:, LANES:]

    cqn = _rms_norm(proj(_O_CQ, _O_CKV), qg_ref[...]).astype(BF16)
    q3 = jnp.dot(cqn, wuq_ref[...], preferred_element_type=F32)
    hw = MLA_HEADS * LANES
    for h in range(MLA_HEADS):
        nope = q3[:, h * LANES:(h + 1) * LANES].astype(BF16)
        lat = jnp.dot(nope, wuk_ref[h], preferred_element_type=F32) * Q_SCALE
        rot = (q3[:, hw + h * LANES:hw + (h + 1) * LANES] * cos_t
               + q3[:, 2 * hw + h * LANES:2 * hw + (h + 1) * LANES] * sin_t) * Q_SCALE
        q_ref[h, :, :MLA_RANK] = lat.astype(BF16)
        q_ref[h, :, MLA_RANK:] = rot.astype(BF16)

    ckv = _rms_norm(proj(_O_CKV, _O_HQ), kvg_ref[...])
    ckv_ref[...] = ckv
    kv_ref[:, :MLA_RANK] = ckv.astype(BF16)
    zk = proj(_O_KR, _O_END)
    krot = zk[:, :LANES] * cos_t + zk[:, LANES:] * sin_t
    kr_ref[...] = krot[:, :MLA_D_ROPE]
    kv_ref[:, MLA_RANK:] = krot.astype(BF16)

    hq_ref[...] = proj(_O_HQ, _O_HF)
    zf = proj(_O_HF, _O_HI)
    log_lb = lbp_ref[0:1, :]
    log_1m = lbp_ref[1:2, :]
    one_m = lbp_ref[2:3, :]
    c = log_1m + _log_sigmoid(zf)
    hlf_ref[...] = jnp.maximum(log_lb, c) + jnp.log(1.0 + jnp.exp(-jnp.abs(log_lb - c)))
    hk_ref[...] = one_m * _sigmoid(-zf)
    hv_ref[...] = proj(_O_HI, _O_HG).astype(BF16)
    hgt_ref[...] = _silu(proj(_O_HG, _O_CU)).astype(BF16)

    u_ref[...] = _gelu(proj(_O_CU, _O_CV)).astype(BF16)
    vcm_ref[...] = _layer_norm(_gelu(proj(_O_CV, _O_GA)), cmg_ref[...], cmb_ref[...]).astype(vcm_ref.dtype)

    sga_ref[...] = _sigmoid(proj(_O_GA, _O_GB)).astype(BF16)
    sgb_ref[...] = _sigmoid(proj(_O_GB, _O_GC)).astype(BF16)
    sgc_ref[...] = _sigmoid(proj(_O_GC, _O_KR)).astype(BF16)


def _proj(x, lw, rope_tab, tm, vcm_dtype):
    m, d = x.shape
    row = lambda w: pl.BlockSpec((tm, w), lambda i: (i, 0))
    out_shape = [
        jax.ShapeDtypeStruct((MLA_HEADS, m, QK_WIDTH), BF16),
        jax.ShapeDtypeStruct((m, QK_WIDTH), BF16),
        jax.ShapeDtypeStruct((m, MLA_RANK), F32),
        jax.ShapeDtypeStruct((m, MLA_D_ROPE), F32),
        jax.ShapeDtypeStruct((m, HG_W), F32),
        jax.ShapeDtypeStruct((m, HG_W), F32),
        jax.ShapeDtypeStruct((m, HG_W), F32),
        jax.ShapeDtypeStruct((m, HG_W), BF16),
        jax.ShapeDtypeStruct((m, HG_W), BF16),
        jax.ShapeDtypeStruct((m, CM_WIDTH), BF16),
        jax.ShapeDtypeStruct((m, CM_WIDTH), vcm_dtype),
        jax.ShapeDtypeStruct((m, d), BF16),
        jax.ShapeDtypeStruct((m, d), BF16),
        jax.ShapeDtypeStruct((m, d), BF16),
    ]
    out_specs = [
        pl.BlockSpec((MLA_HEADS, tm, QK_WIDTH), lambda i: (0, i, 0)),
        row(QK_WIDTH), row(MLA_RANK), row(MLA_D_ROPE),
        row(HG_W), row(HG_W), row(HG_W), row(HG_W), row(HG_W),
        row(CM_WIDTH), row(CM_WIDTH), row(d), row(d), row(d),
    ]
    in_specs = [
        row(d),
        lw.spec("w_in"), lw.spec("w_uq"), lw.spec("w_uk"), lw.spec("q_norm_g"), lw.spec("kv_norm_g"),
        pl.BlockSpec((tm, 2 * LANES), lambda i: (i % (rope_tab.shape[0] // tm), 0)),
        lw.spec("lbp"), lw.spec("cm_ln_g"), lw.spec("cm_ln_b"),
    ]
    return pl.pallas_call(
        _proj_kernel, grid=(m // tm,), in_specs=in_specs, out_specs=out_specs, out_shape=out_shape,
        compiler_params=_params("parallel"), name="proj",
    )(x, lw["w_in"], lw["w_uq"], lw["w_uk"], lw["q_norm_g"], lw["kv_norm_g"], rope_tab,
      lw["lbp"], lw["cm_ln_g"], lw["cm_ln_b"])


def _attn_prompt_kernel(q_ref, k_ref, o_ref, m_ref, l_ref, acc_ref):
    qb = pl.program_id(1)
    tq = q_ref.shape[1]
    rows = MLA_HEADS * tq
    q = q_ref[...].reshape(rows, QK_WIDTH)
    kv_len = (qb + 1) * tq
    nblk = (kv_len + KV_BLOCK - 1) // KV_BLOCK
    rep = KV_BLOCK // LANES
    m_ref[...] = jnp.full(m_ref.shape, MASK_VALUE, F32)
    l_ref[...] = jnp.zeros(l_ref.shape, F32)
    acc_ref[...] = jnp.zeros(acc_ref.shape, F32)

    def scores(j):
        start = pl.multiple_of(j * KV_BLOCK, KV_BLOCK)
        return lax.dot_general(q, k_ref[pl.ds(start, KV_BLOCK), :], (((1,), (1,)), ((), ())),
                               preferred_element_type=F32)

    def update(j, s):
        start = pl.multiple_of(j * KV_BLOCK, KV_BLOCK)
        m_prev = m_ref[...]
        m_new = jnp.maximum(m_prev, jnp.max(s, -1, keepdims=True))
        alpha = jnp.exp2(m_prev - m_new)
        p = jnp.exp2(s - jnp.tile(m_new, (1, rep)))
        l_ref[...] = alpha * l_ref[...] + jnp.sum(p, -1, keepdims=True)
        acc_ref[...] = jnp.tile(alpha, (1, MLA_RANK // LANES)) * acc_ref[...] + jnp.dot(
            p.astype(BF16), k_ref[pl.ds(start, KV_BLOCK), :MLA_RANK], preferred_element_type=F32)
        m_ref[...] = m_new

    def body(j, s):
        s_next = scores(j + 1)
        update(j, s)
        return s_next

    s_last = lax.fori_loop(0, nblk - 1, body, scores(0))
    r = lax.broadcasted_iota(jnp.int32, s_last.shape, 0)
    q_tok = qb * tq + (r & (tq - 1))
    k_idx = (nblk - 1) * KV_BLOCK + lax.broadcasted_iota(jnp.int32, s_last.shape, 1)
    update(nblk - 1, jnp.where(k_idx < ((q_tok // CHUNK) + 1) * CHUNK, s_last, MASK_VALUE))
    o = acc_ref[...] * jnp.tile(1.0 / l_ref[...], (1, MLA_RANK // LANES))
    for h in range(MLA_HEADS):
        o_ref[:, h * MLA_RANK:(h + 1) * MLA_RANK] = o[h * tq:(h + 1) * tq, :].astype(BF16)


def _attn_prompt(q, k, batch, seq, tq):
    m = batch * seq
    nqc = seq // tq
    rows = MLA_HEADS * tq
    assert tq & (tq - 1) == 0 and tq % CHUNK == 0 and seq % tq == 0
    return pl.pallas_call(
        _attn_prompt_kernel, grid=(batch, nqc),
        in_specs=[pl.BlockSpec((MLA_HEADS, tq, QK_WIDTH), lambda b, c: (0, b * nqc + c, 0)),
                  pl.BlockSpec((seq, QK_WIDTH), lambda b, c: (b, 0))],
        out_specs=pl.BlockSpec((tq, MLA_HEADS * MLA_RANK), lambda b, c: (b * nqc + c, 0)),
        out_shape=jax.ShapeDtypeStruct((m, MLA_HEADS * MLA_RANK), BF16),
        scratch_shapes=[pltpu.VMEM((rows, LANES), F32), pltpu.VMEM((rows, LANES), F32),
                        pltpu.VMEM((rows, MLA_RANK), F32)],
        compiler_params=_params("parallel", "arbitrary"), name="attn_prompt",
    )(q, k)


def _attn_sample_kernel(past_len, q_ref, k_ref, cckv_ref, ckr_ref, o_ref):
    nq = q_ref.shape[1]
    rows = MLA_HEADS * nq
    q = q_ref[...].reshape(rows, QK_WIDTH)
    q_lat = q[:, :MLA_RANK]
    q_rot = q[:, MLA_RANK:MLA_RANK + MLA_D_ROPE]
    nt = (((1,), (1,)), ((), ()))
    kc = cckv_ref[...].astype(BF16)
    s1 = (lax.dot_general(q_lat, kc, nt, preferred_element_type=F32)
          + lax.dot_general(q_rot, ckr_ref[...].astype(BF16), nt, preferred_element_type=F32))
    kn = k_ref[...]
    s2 = lax.dot_general(q, kn, nt, preferred_element_type=F32)
    q_pos = past_len + lax.broadcasted_iota(jnp.int32, s2.shape, 0) % nq
    k_pos = past_len + lax.broadcasted_iota(jnp.int32, s2.shape, 1)
    s2 = jnp.where(k_pos // CHUNK <= q_pos // CHUNK, s2, MASK_VALUE)
    mx = jnp.maximum(jnp.max(s1, -1, keepdims=True), jnp.max(s2, -1, keepdims=True))
    p1 = jnp.exp2(s1 - mx)
    p2 = jnp.exp2(s2 - mx)
    l = jnp.sum(p1, -1, keepdims=True) + jnp.sum(p2, -1, keepdims=True)
    o = (jnp.dot(p1.astype(BF16), kc, preferred_element_type=F32)
         + jnp.dot(p2.astype(BF16), kn[:, :MLA_RANK], preferred_element_type=F32)) * (1.0 / l)
    for h in range(MLA_HEADS):
        o_ref[:, h * MLA_RANK:(h + 1) * MLA_RANK] = o[h * nq:(h + 1) * nq, :].astype(BF16)


def _attn_sample(q, k, cache_ckv, cache_kr, layer, batch, nq):
    past_len = cache_ckv.shape[2]
    return pl.pallas_call(
        functools.partial(_attn_sample_kernel, past_len), grid=(batch,),
        in_specs=[pl.BlockSpec((MLA_HEADS, nq, QK_WIDTH), lambda b: (0, b, 0)),
                  pl.BlockSpec((nq, QK_WIDTH), lambda b: (b, 0)),
                  pl.BlockSpec((None, None, past_len, MLA_RANK), lambda b: (layer, b, 0, 0)),
                  pl.BlockSpec((None, None, past_len, MLA_D_ROPE), lambda b: (layer, b, 0, 0))],
        out_specs=pl.BlockSpec((nq, MLA_HEADS * MLA_RANK), lambda b: (b, 0)),
        out_shape=jax.ShapeDtypeStruct((batch * nq, MLA_HEADS * MLA_RANK), BF16),
        compiler_params=_params("parallel"), name="attn_sample",
    )(q, k, cache_ckv, cache_kr)


def _hgrn_levels(lc):
    spans = []
    sp = lc // 2
    while sp >= 1:
        spans.append(sp)
        sp //= 2
    return spans


def _hgrn_consts(lc):
    spans = _hgrn_levels(lc)
    tri = np.tril(np.ones((lc, lc), np.float32))
    t = np.arange(lc)
    mats, masks = [tri], []
    for sp in spans:
        blk = t // (2 * sp)
        ref_row = blk * 2 * sp + sp - 1
        mats.append(np.abs(tri - tri[ref_row]))
        right = (t % (2 * sp)) >= sp
        masks.append(((blk[:, None] == blk[None, :]) & right[:, None] & ~right[None, :]).astype(np.float32))
    return np.tile(np.concatenate(mats, 0), (1, 3)), np.stack(masks, 0)


def _hgrn_kernel(has_init, q_ref, k_ref, lf_ref, v_ref, gt_ref, gn_ref, cmat_ref, mask_ref, *rest):
    s0_ref = rest[0] if has_init else None
    o_ref, sout_ref, st_ref = rest[-3:]
    c = pl.program_id(1)
    nb, lc, _ = q_ref.shape
    nlev = mask_ref.shape[0]
    nt = (((1,), (1,)), ((), ()))
    tn = (((0,), (0,)), ((), ()))

    @pl.when(c == 0)
    def _():
        for bi in range(nb):
            for h in range(HG_HEADS):
                if has_init:
                    st_ref[bi, h] = s0_ref[bi, h].T
                else:
                    st_ref[bi, h] = jnp.zeros((HG_DV, HG_DK), F32)

    eye = (lax.broadcasted_iota(jnp.int32, (lc, lc), 0) == lax.broadcasted_iota(jnp.int32, (lc, lc), 1))
    lvl_mask = [mask_ref[lv] > 0.5 for lv in range(nlev)]
    for bi in range(nb):
        lf = lf_ref[bi] * LOG2E
        lf_hi = lf.astype(BF16)
        r1 = lf - lf_hi.astype(F32)
        lf_mid = r1.astype(BF16)
        lf_lo = (r1 - lf_mid.astype(F32)).astype(BF16)
        br = jnp.dot(cmat_ref[...], jnp.concatenate([lf_hi, lf_mid, lf_lo], 0), preferred_element_type=F32)
        for h in range(HG_HEADS):
            sl = slice(h * HG_DK, (h + 1) * HG_DK)
            bh = br[:lc, sl]
            qh = q_ref[bi, :, sl]
            kh = k_ref[bi, :, sl]
            vh = v_ref[bi, :, sl]
            a = jnp.where(eye, jnp.sum(qh * kh, -1, keepdims=True), 0.0)
            for lv in range(nlev):
                e = jnp.exp2(br[(1 + lv) * lc:(2 + lv) * lc, sl])
                a = jnp.where(lvl_mask[lv],
                              lax.dot_general((qh * e).astype(BF16), (kh * e).astype(BF16), nt,
                                              preferred_element_type=F32), a)
            st = st_ref[bi, h]
            o = (jnp.dot(a.astype(BF16), vh, preferred_element_type=F32)
                 + lax.dot_general((qh * jnp.exp2(bh)).astype(BF16), st.astype(BF16), nt,
                                   preferred_element_type=F32))
            b_last = bh[lc - 1:lc, :]
            kdec = (kh * jnp.exp2(b_last - bh)).astype(BF16)
            st_ref[bi, h] = st * jnp.exp2(b_last) + lax.dot_general(vh, kdec, tn, preferred_element_type=F32)
            on = _rms_norm(o, gn_ref[:, sl])
            o_ref[bi, :, sl] = (on * gt_ref[bi, :, sl].astype(F32)).astype(BF16)

    @pl.when(c == pl.num_programs(1) - 1)
    def _():
        for bi in range(nb):
            for h in range(HG_HEADS):
                sout_ref[bi, h] = st_ref[bi, h].T


def _hgrn(hq, hk, hlf, hv, hgt, lw, state, batch, seq, lc, nb):
    nch = seq // lc
    assert batch % nb == 0
    cmat, masks = _hgrn_consts(lc)
    has_init = state is not None
    layer = lw.layer
    row = pl.BlockSpec((nb, lc, HG_W), lambda b, c: (b, c, 0))
    st_spec = pl.BlockSpec((nb, HG_HEADS, HG_DK, HG_DV), lambda b, c: (b, 0, 0, 0))
    in_specs = [row, row, row, row, row, lw.spec("hg_norm_g"), _const_spec(cmat.shape), _const_spec(masks.shape)]
    args = [t.reshape(batch, seq, HG_W) for t in (hq, hk, hlf, hv, hgt)]
    args += [lw["hg_norm_g"], jnp.asarray(cmat, BF16), jnp.asarray(masks)]
    if has_init:
        in_specs.append(pl.BlockSpec((None, nb, HG_HEADS, HG_DK, HG_DV), lambda b, c: (layer, b, 0, 0, 0)))
        args.append(state)
    o, s_fin = pl.pallas_call(
        functools.partial(_hgrn_kernel, has_init), grid=(batch // nb, nch),
        in_specs=in_specs, out_specs=[row, st_spec],
        out_shape=[jax.ShapeDtypeStruct((batch, seq, HG_W), BF16),
                   jax.ShapeDtypeStruct((batch, HG_HEADS, HG_DK, HG_DV), F32)],
        scratch_shapes=[pltpu.VMEM((nb, HG_HEADS, HG_DV, HG_DK), F32)],
        compiler_params=_params("parallel", "arbitrary"), name="hgrn",
    )(*args)
    return o.reshape(batch * seq, HG_W), s_fin


def _merge_kernel(alpha, cml, x_ref, ol_ref, hg_ref, u_ref, vcm_ref, sga_ref, sgb_ref, sgc_ref,
                  wuv_ref, wpa_ref, wpb_ref, wpc_ref, wo_ref, ws_ref, bs_ref, g_ref, b_ref,
                  o_ref, cin_ref):
    tm = x_ref.shape[0]
    tril = (lax.broadcasted_iota(jnp.int32, (cml, cml), 0) >= lax.broadcasted_iota(jnp.int32, (cml, cml), 1))
    for g in range(CM_GROUPS):
        sl = slice(g * CM_GROUP_DIM, (g + 1) * CM_GROUP_DIM)
        wg = jnp.where(tril, ws_ref[g, :cml, :cml], 0.0).astype(BF16)
        for r in range(tm // cml):
            rs = slice(r * cml, (r + 1) * cml)
            s = jnp.dot(wg, vcm_ref[rs, sl].astype(BF16), preferred_element_type=F32) + bs_ref[:cml, sl]
            cin_ref[rs, sl] = (u_ref[rs, sl].astype(F32) * s).astype(BF16)
    a_in = jnp.dot(ol_ref[...], wuv_ref[...], preferred_element_type=F32).astype(BF16)
    y_a = jnp.dot(a_in, wpa_ref[...], preferred_element_type=F32)
    y_b = jnp.dot(hg_ref[...], wpb_ref[...], preferred_element_type=F32)
    y_c = jnp.dot(cin_ref[...], wpc_ref[...], preferred_element_type=F32)
    mrg = (sga_ref[...].astype(F32) * y_a + sgb_ref[...].astype(F32) * y_b
           + sgc_ref[...].astype(F32) * y_c).astype(BF16)
    y = alpha * x_ref[...] + jnp.dot(mrg, wo_ref[...], preferred_element_type=F32)
    o_ref[...] = _layer_norm(y, g_ref[...], b_ref[...])


def _merge(x, o_lat, hg_o, u, vcm, sga, sgb, sgc, lw, alpha, cml, tm):
    m, d = x.shape
    row = lambda w: pl.BlockSpec((tm, w), lambda i: (i, 0))
    in_specs = [row(d), row(MLA_HEADS * MLA_RANK), row(HG_W), row(CM_WIDTH), row(CM_WIDTH), row(d), row(d), row(d),
                lw.spec("w_uv"), lw.spec("w_pa"), lw.spec("w_pb"), lw.spec("w_pc"), lw.spec("w_o"),
                lw.spec("cm_ws"), lw.spec("cm_bs_full"), lw.spec("ln1_g"), lw.spec("ln1_b")]
    return pl.pallas_call(
        functools.partial(_merge_kernel, alpha, cml), grid=(m // tm,), in_specs=in_specs,
        out_specs=row(d), out_shape=jax.ShapeDtypeStruct((m, d), F32),
        scratch_shapes=[pltpu.VMEM((tm, CM_WIDTH), BF16)],
        compiler_params=_params("parallel"), name="merge",
    )(x, o_lat, hg_o, u, vcm, sga, sgb, sgc, lw["w_uv"], lw["w_pa"], lw["w_pb"], lw["w_pc"], lw["w_o"],
      lw["cm_ws"], lw["cm_bs_full"], lw["ln1_g"], lw["ln1_b"])


def _ffn_kernel(alpha, nsplit, x_ref, wup_ref, wdn_ref, g_ref, b_ref, o_ref):
    x = x_ref[...]
    xb = x.astype(BF16)
    dff = wdn_ref.shape[0]
    cw = dff // nsplit
    y = alpha * x
    for j in range(nsplit):
        gate = jnp.dot(xb, wup_ref[:, j * cw:(j + 1) * cw], preferred_element_type=F32)
        up = jnp.dot(xb, wup_ref[:, dff + j * cw:dff + (j + 1) * cw], preferred_element_type=F32)
        act = (_silu(gate) * up).astype(BF16)
        y = y + jnp.dot(act, wdn_ref[j * cw:(j + 1) * cw, :], preferred_element_type=F32)
    o_ref[...] = _layer_norm(y, g_ref[...], b_ref[...])


def _ffn(x, lw, alpha, tm):
    m, d = x.shape
    row = pl.BlockSpec((tm, d), lambda i: (i, 0))
    return pl.pallas_call(
        functools.partial(_ffn_kernel, alpha, 2), grid=(m // tm,),
        in_specs=[row, lw.spec("w_up"), lw.spec("w_down"), lw.spec("ln2_g"), lw.spec("ln2_b")],
        out_specs=row, out_shape=jax.ShapeDtypeStruct((m, d), F32),
        compiler_params=_params("parallel"), name="ffn",
    )(x, lw["w_up"], lw["w_down"], lw["ln2_g"], lw["ln2_b"])


def _rope_table(pos):
    half = MLA_D_ROPE // 2
    inv = 1.0 / (ROPE_THETA ** (jnp.arange(half, dtype=F32) / half))
    ang = pos.astype(F32)[:, None] * inv[None]
    cos, sin = jnp.cos(ang), jnp.sin(ang)
    z = jnp.zeros((pos.shape[0], LANES - MLA_D_ROPE), F32)
    return jnp.concatenate([cos, cos, z, -sin, sin, z], -1)


def _prep_weights(w_in, q_norm_g, w_uq, w_uk, kv_norm_g, w_uv, hg_lb, hg_norm_g, cm_ln_g, cm_ln_b,
                  cm_ws, cm_bs, w_pa, w_pb, w_pc, w_o, ln1_g, ln1_b, w_up, w_down, ln2_g, ln2_b):
    depth, d, _ = w_in.shape
    half = MLA_D_ROPE // 2
    cols = np.cumsum([0, MLA_RANK, MLA_RANK, MLA_D_ROPE, HG_W, HG_W, HG_W, HG_W, CM_WIDTH, CM_WIDTH, d, d, d])
    seg = lambda i: w_in[:, :, cols[i]:cols[i + 1]]
    kr = seg(2)
    zk = jnp.zeros((depth, d, LANES - MLA_D_ROPE), F32)
    w_in_r = jnp.concatenate(
        [seg(0), seg(1)] + [seg(i) for i in range(3, 12)]
        + [kr, zk, kr[..., half:], kr[..., :half], zk], -1).astype(BF16)

    dq = MLA_D_NOPE + MLA_D_ROPE
    uq = w_uq.reshape(depth, MLA_RANK, MLA_HEADS, dq)
    nope = uq[..., :MLA_D_NOPE]
    r1, r2 = uq[..., MLA_D_NOPE:MLA_D_NOPE + half], uq[..., MLA_D_NOPE + half:]
    zn = jnp.zeros((depth, MLA_RANK, MLA_HEADS, LANES - MLA_D_NOPE), F32)
    zr = jnp.zeros((depth, MLA_RANK, MLA_HEADS, LANES - MLA_D_ROPE), F32)
    flat = lambda t: t.reshape(depth, MLA_RANK, MLA_HEADS * LANES)
    w_uq_r = jnp.concatenate([flat(jnp.concatenate([nope, zn], -1)),
                              flat(jnp.concatenate([r1, r2, zr], -1)),
                              flat(jnp.concatenate([r2, r1, zr], -1))], -1).astype(BF16)
    ukt = jnp.transpose(w_uk, (0, 2, 3, 1))
    w_uk_r = jnp.concatenate([ukt, jnp.zeros((depth, MLA_HEADS, LANES - MLA_D_NOPE, MLA_RANK), F32)], 2).astype(BF16)
    uvt = jnp.transpose(w_uv, (0, 2, 1, 3))
    eye = jnp.eye(MLA_HEADS, dtype=F32)
    w_uv_r = (uvt[:, :, :, None, :] * eye[None, :, None, :, None]).reshape(
        depth, MLA_HEADS * MLA_RANK, MLA_HEADS * MLA_D_V).astype(BF16)

    sm = jax.nn.softmax(hg_lb.astype(F32), axis=0)
    lb = jnp.concatenate([jnp.zeros_like(sm[:1]), jnp.cumsum(sm[1:], axis=0)], axis=0)
    lbp = jnp.stack([jnp.log(lb + LB_TINY), jnp.log1p(-lb), 1.0 - lb], 1)

    bs_full = jnp.repeat(jnp.transpose(cm_bs, (0, 2, 1)), CM_GROUP_DIM, axis=-1)

    r3 = lambda t: t.reshape(depth, 1, -1)
    return dict(w_in=w_in_r, w_uq=w_uq_r, w_uk=w_uk_r, w_uv=w_uv_r, lbp=lbp,
                q_norm_g=r3(q_norm_g), kv_norm_g=r3(kv_norm_g), hg_norm_g=r3(hg_norm_g),
                cm_ln_g=r3(cm_ln_g), cm_ln_b=r3(cm_ln_b), cm_ws=cm_ws, cm_bs_full=bs_full,
                w_pa=w_pa.astype(BF16), w_pb=w_pb.astype(BF16), w_pc=w_pc.astype(BF16), w_o=w_o.astype(BF16),
                ln1_g=r3(ln1_g), ln1_b=r3(ln1_b), w_up=w_up.astype(BF16), w_down=w_down.astype(BF16),
                ln2_g=r3(ln2_g), ln2_b=r3(ln2_b))


def _hgrn_rows(batch):
    for nb in (4, 2, 1):
        if batch % nb == 0:
            return nb


def _row_tile(m, want):
    tm = min(m, want)
    assert m % tm == 0, (m, tm)
    return tm


def _layer(x, lw, rope_tab, alpha, batch, seq, past, tm):
    prompt = past is None
    outs = _proj(x, lw, rope_tab, tm, BF16 if prompt else F32)
    q, k, ckv, kr, hq, hk, hlf, hv, hgt, u, vcm, sga, sgb, sgc = outs
    if prompt:
        o_lat = _attn_prompt(q, k, batch, seq, Q_BLOCK)
        hg_o, s_fin = _hgrn(hq, hk, hlf, hv, hgt, lw, None, batch, seq, CHUNK, _hgrn_rows(batch))
    else:
        o_lat = _attn_sample(q, k, past[0], past[1], lw.layer, batch, seq)
        hg_o, s_fin = _hgrn(hq, hk, hlf, hv, hgt, lw, past[2], batch, seq, seq, _hgrn_rows(batch))
    cml = min(seq, CM_CHUNK)
    x = _merge(x, o_lat, hg_o, u, vcm, sga, sgb, sgc, lw, alpha, cml, tm)
    x = _ffn(x, lw, alpha, tm)
    return x, (ckv, kr, s_fin, vcm)


def kernel(x_prompt, x_sample, cache_mla_ckv, cache_mla_krope, state_hgrn, emb_ln_g, emb_ln_b, w_in, q_norm_g, w_uq, w_uk, kv_norm_g, w_uv, hg_lb, hg_norm_g, cm_ln_g, cm_ln_b, cm_ws, cm_bs, w_pa, w_pb, w_pc, w_o, ln1_g, ln1_b, w_up, w_down, ln2_g, ln2_b):
    bp, sp, d = x_prompt.shape
    bs, ss, _ = x_sample.shape
    depth = w_in.shape[0]
    past_len = cache_mla_ckv.shape[2]
    assert sp % CM_CHUNK == 0 and sp % KV_BLOCK == 0 and ss <= CHUNK and ss % 16 == 0
    alpha = float((2 * depth) ** 0.25)

    wts = _prep_weights(w_in, q_norm_g, w_uq, w_uk, kv_norm_g, w_uv, hg_lb, hg_norm_g, cm_ln_g, cm_ln_b,
                        cm_ws, cm_bs, w_pa, w_pb, w_pc, w_o, ln1_g, ln1_b, w_up, w_down, ln2_g, ln2_b)
    tm_p = _row_tile(bp * sp, 256)
    tm_s = _row_tile(bs * ss, 256)
    assert (sp % tm_p == 0 or tm_p % sp == 0) and (ss % tm_s == 0 or tm_s % ss == 0)
    rope_p = jnp.tile(_rope_table(jnp.arange(sp, dtype=jnp.int32)), (max(1, tm_p // sp), 1))
    rope_s = jnp.tile(_rope_table(past_len + jnp.arange(ss, dtype=jnp.int32)), (max(1, tm_s // ss), 1))
    eg, eb = emb_ln_g.reshape(1, d), emb_ln_b.reshape(1, d)
    xp = _input_ln(x_prompt.reshape(bp * sp, d), eg, eb, tm_p)
    xs = _input_ln(x_sample.reshape(bs * ss, d), eg, eb, tm_s)

    acc = [[] for _ in range(7)]
    for l in range(depth):
        lw = _LayerWeights(wts, l)
        xp, (ckv_p, kr_p, s_p, _) = _layer(xp, lw, rope_p, alpha, bp, sp, None, tm_p)
        xs, (ckv_s, kr_s, s_s, v_s) = _layer(
            xs, lw, rope_s, alpha, bs, ss, (cache_mla_ckv, cache_mla_krope, state_hgrn), tm_s)
        for lst, val in zip(acc, (ckv_p.reshape(bp, sp, -1), kr_p.reshape(bp, sp, -1), s_p,
                                  ckv_s.reshape(bs, ss, -1), kr_s.reshape(bs, ss, -1), s_s,
                                  v_s.reshape(bs, ss, -1))):
            lst.append(val)
    return (xp.reshape(bp, sp, d), xs.reshape(bs, ss, d)) + tuple(jnp.stack(a) for a in acc)
```

```python
import functools

import numpy as np
import jax
import jax.numpy as jnp
from jax import lax
from jax.experimental import pallas as pl
from jax.experimental.pallas import tpu as pltpu

F32 = jnp.float32
BF16 = jnp.bfloat16

CHUNK = 64
MLA_HEADS = 8
MLA_D_NOPE = 64
MLA_D_ROPE = 32
MLA_D_V = 64
MLA_RANK = 256
MLA_SCALE = (MLA_D_NOPE + MLA_D_ROPE) ** -0.5
LOG2E = float(np.log2(np.e))
Q_SCALE = MLA_SCALE * LOG2E
ROPE_THETA = 10000.0
MASK_VALUE = -1e30
HG_HEADS = 4
HG_DK = 128
HG_DV = 128
HG_W = HG_HEADS * HG_DK
LB_TINY = 1e-30
CM_CHUNK = 128
CM_GROUPS = 4
CM_WIDTH = 512
CM_GROUP_DIM = CM_WIDTH // CM_GROUPS
EPS = 1e-5

LANES = 128
VMEM_LIMIT_BYTES = 56 * 1024 * 1024
QK_WIDTH = MLA_RANK + LANES
KV_BLOCK = 256
Q_BLOCK = 128

_A_CQ, _A_CKV, _A_END = 0, 256, 512
_B_HQ, _B_HF, _B_HI, _B_HG, _B_CU, _B_CV, _B_GA, _B_GB, _B_GC, _B_END = (
    0, 512, 1024, 1536, 2048, 2560, 3072, 4096, 5120, 6144)


def _const_spec(shape):
    nd = len(shape)
    return pl.BlockSpec(shape, lambda *_: (0,) * nd, pipeline_mode=pl.Buffered(1))


class _LayerWeights:
    def __init__(self, stacked, layer):
        self.stacked = stacked
        self.layer = layer

    def __getitem__(self, name):
        return self.stacked[name]

    def spec(self, name):
        shape = self.stacked[name].shape[1:]
        layer, nd = self.layer, len(shape)
        return pl.BlockSpec((None,) + shape, lambda *_: (layer,) + (0,) * nd, pipeline_mode=pl.Buffered(1))


def _params(*sem):
    return pltpu.CompilerParams(dimension_semantics=sem, vmem_limit_bytes=VMEM_LIMIT_BYTES)


def _layer_norm(x, g, b):
    mu = jnp.mean(x, -1, keepdims=True)
    xc = x - mu
    var = jnp.mean(xc * xc, -1, keepdims=True)
    return xc * lax.rsqrt(var + EPS) * g + b


def _rms_norm(x, g):
    return x * lax.rsqrt(jnp.mean(x * x, -1, keepdims=True) + EPS) * g


def _gelu(x):
    return 0.5 * x * (1.0 + lax.erf(x * np.float32(1.0 / np.sqrt(2.0))))


def _sigmoid(x):
    return 1.0 / (1.0 + jnp.exp(-x))


def _silu(x):
    return x * _sigmoid(x)


def _log_sigmoid(x):
    return jnp.minimum(x, 0.0) - jnp.log(1.0 + jnp.exp(-jnp.abs(x)))


def _ln_kernel(x_ref, g_ref, b_ref, o_ref):
    o_ref[...] = _layer_norm(x_ref[...], g_ref[...], b_ref[...])


def _input_ln(x, g, b, tm):
    m, d = x.shape
    return pl.pallas_call(
        _ln_kernel, grid=(m // tm,),
        in_specs=[pl.BlockSpec((tm, d), lambda i: (i, 0)), _const_spec((1, d)), _const_spec((1, d))],
        out_specs=pl.BlockSpec((tm, d), lambda i: (i, 0)),
        out_shape=jax.ShapeDtypeStruct((m, d), F32),
        compiler_params=_params("parallel"), name="input_ln",
    )(x, g, b)


def _proj_kernel(n_alias, x_ref, wa_ref, wkr_ref, wb_ref, wuq_ref, wuk_ref, qg_ref, kvg_ref, rope_ref,
                 lbp_ref, cmg_ref, cmb_ref, *rest):
    (q_ref, kv_ref, ckv_ref, kr_ref, hq_ref, hk_ref, hlf_ref, hv_ref, hgt_ref,
     u_ref, vcm_ref, sga_ref, sgb_ref, sgc_ref) = rest[n_alias:]
    xb = x_ref[...].astype(BF16)

    def proj_a(lo, hi):
        return jnp.dot(xb, wa_ref[:, lo:hi], preferred_element_type=F32)

    def proj(lo, hi):
        return jnp.dot(xb, wb_ref[:, lo:hi], preferred_element_type=F32)

    cos_t = rope_ref[:, :LANES]
    sin_t = rope_ref[:, LANES:]

    cqn = _rms_norm(proj_a(_A_CQ, _A_CKV), qg_ref[...]).astype(BF16)
    q3 = jnp.dot(cqn, wuq_ref[...], preferred_element_type=F32)
    hw = MLA_HEADS * LANES
    for h in range(MLA_HEADS):
        nope = q3[:, h * LANES:(h + 1) * LANES].astype(BF16)
        lat = jnp.dot(nope, wuk_ref[h], preferred_element_type=F32) * Q_SCALE
        rot = (q3[:, hw + h * LANES:hw + (h + 1) * LANES] * cos_t
               + q3[:, 2 * hw + h * LANES:2 * hw + (h + 1) * LANES] * sin_t) * Q_SCALE
        q_ref[h, :, :MLA_RANK] = lat.astype(BF16)
        q_ref[h, :, MLA_RANK:] = rot.astype(BF16)

    ckv = _rms_norm(proj_a(_A_CKV, _A_END), kvg_ref[...])
    ckv_ref[...] = ckv
    kv_ref[:, :MLA_RANK] = ckv.astype(BF16)
    zk = jnp.dot(xb, wkr_ref[...], preferred_element_type=F32)
    krot = zk[:, :LANES] * cos_t + zk[:, LANES:] * sin_t
    kr_ref[...] = krot[:, :MLA_D_ROPE]
    kv_ref[:, MLA_RANK:] = krot.astype(BF16)

    hq_ref[...] = proj(_B_HQ, _B_HF)
    zf = proj(_B_HF, _B_HI)
    log_lb = lbp_ref[0:1, :]
    log_1m = lbp_ref[1:2, :]
    one_m = lbp_ref[2:3, :]
    c = log_1m + _log_sigmoid(zf)
    hlf_ref[...] = jnp.maximum(log_lb, c) + jnp.log(1.0 + jnp.exp(-jnp.abs(log_lb - c)))
    hk_ref[...] = one_m * _sigmoid(-zf)
    hv_ref[...] = proj(_B_HI, _B_HG).astype(BF16)
    hgt_ref[...] = _silu(proj(_B_HG, _B_CU)).astype(BF16)

    u_ref[...] = _gelu(proj(_B_CU, _B_CV)).astype(BF16)
    vcm_ref[...] = _layer_norm(_gelu(proj(_B_CV, _B_GA)), cmg_ref[...], cmb_ref[...]).astype(vcm_ref.dtype)

    sga_ref[...] = _sigmoid(proj(_B_GA, _B_GB)).astype(BF16)
    sgb_ref[...] = _sigmoid(proj(_B_GB, _B_GC)).astype(BF16)
    sgc_ref[...] = _sigmoid(proj(_B_GC, _B_END)).astype(BF16)


def _proj(x, lw, rope_tab, tm, vcm_dtype, prev):
    m, d = x.shape
    depth = lw["w_b"].shape[0]
    layer = lw.layer
    row = lambda w: pl.BlockSpec((tm, w), lambda i: (i, 0))
    lrow = lambda w: pl.BlockSpec((None, tm, w), lambda i: (layer, i, 0))
    out_shape = [
        jax.ShapeDtypeStruct((MLA_HEADS, m, QK_WIDTH), BF16),
        jax.ShapeDtypeStruct((m, QK_WIDTH), BF16),
        jax.ShapeDtypeStruct((depth, m, MLA_RANK), F32),
        jax.ShapeDtypeStruct((depth, m, MLA_D_ROPE), F32),
        jax.ShapeDtypeStruct((m, HG_W), F32),
        jax.ShapeDtypeStruct((m, HG_W), F32),
        jax.ShapeDtypeStruct((m, HG_W), F32),
        jax.ShapeDtypeStruct((m, HG_W), BF16),
        jax.ShapeDtypeStruct((m, HG_W), BF16),
        jax.ShapeDtypeStruct((m, CM_WIDTH), BF16),
        jax.ShapeDtypeStruct((m, CM_WIDTH), vcm_dtype),
        jax.ShapeDtypeStruct((m, d), BF16),
        jax.ShapeDtypeStruct((m, d), BF16),
        jax.ShapeDtypeStruct((m, d), BF16),
    ]
    out_specs = [
        pl.BlockSpec((MLA_HEADS, tm, QK_WIDTH), lambda i: (0, i, 0)),
        row(QK_WIDTH), lrow(MLA_RANK), lrow(MLA_D_ROPE),
        row(HG_W), row(HG_W), row(HG_W), row(HG_W), row(HG_W),
        row(CM_WIDTH), row(CM_WIDTH), row(d), row(d), row(d),
    ]
    names = ("w_a", "w_kr", "w_b", "w_uq", "w_uk", "q_norm_g", "kv_norm_g")
    tail = ("lbp", "cm_ln_g", "cm_ln_b")
    in_specs = ([row(d)] + [lw.spec(n) for n in names]
                + [pl.BlockSpec((tm, 2 * LANES), lambda i: (i % (rope_tab.shape[0] // tm), 0))]
                + [lw.spec(n) for n in tail])
    args = [x] + [lw[n] for n in names] + [rope_tab] + [lw[n] for n in tail]
    aliases = {}
    if prev is not None:
        aliases = {len(args): 2, len(args) + 1: 3}
        in_specs += [pl.BlockSpec(memory_space=pl.ANY)] * 2
        args += list(prev)
    return pl.pallas_call(
        functools.partial(_proj_kernel, len(aliases)), grid=(m // tm,), in_specs=in_specs,
        out_specs=out_specs, out_shape=out_shape, input_output_aliases=aliases,
        compiler_params=_params("parallel"), name="proj",
    )(*args)


def _uv_project(o, wuv_ref, o_ref, tq):
    ob = o.astype(BF16)
    for i in range(MLA_HEADS // 2):
        pair = (jnp.dot(ob[2 * i * tq:(2 * i + 1) * tq], wuv_ref[2 * i], preferred_element_type=F32)
                + jnp.dot(ob[(2 * i + 1) * tq:(2 * i + 2) * tq], wuv_ref[2 * i + 1], preferred_element_type=F32))
        o_ref[:, i * LANES:(i + 1) * LANES] = pair.astype(BF16)


def _attn_prompt_kernel(q_ref, k_ref, wuv_ref, o_ref, m_ref, l_ref, acc_ref):
    qb = pl.program_id(1)
    tq = q_ref.shape[1]
    rows = MLA_HEADS * tq
    q = q_ref[...].reshape(rows, QK_WIDTH)
    kv_len = (qb + 1) * tq
    nblk = (kv_len + KV_BLOCK - 1) // KV_BLOCK
    rep = KV_BLOCK // LANES
    m_ref[...] = jnp.full(m_ref.shape, MASK_VALUE, F32)
    l_ref[...] = jnp.zeros(l_ref.shape, F32)
    acc_ref[...] = jnp.zeros(acc_ref.shape, F32)

    def scores(j):
        start = pl.multiple_of(j * KV_BLOCK, KV_BLOCK)
        return lax.dot_general(q, k_ref[pl.ds(start, KV_BLOCK), :], (((1,), (1,)), ((), ())),
                               preferred_element_type=F32)

    def update(j, s):
        start = pl.multiple_of(j * KV_BLOCK, KV_BLOCK)
        m_prev = m_ref[...]
        m_new = jnp.maximum(m_prev, jnp.max(s, -1, keepdims=True))
        alpha = jnp.exp2(m_prev - m_new)
        p = jnp.exp2(s - jnp.tile(m_new, (1, rep)))
        l_ref[...] = alpha * l_ref[...] + sum(p[:, i * LANES:(i + 1) * LANES] for i in range(rep))
        acc_ref[...] = jnp.tile(alpha, (1, MLA_RANK // LANES)) * acc_ref[...] + jnp.dot(
            p.astype(BF16), k_ref[pl.ds(start, KV_BLOCK), :MLA_RANK], preferred_element_type=F32)
        m_ref[...] = m_new

    def body(j, s):
        s_next = scores(j + 1)
        update(j, s)
        return s_next

    s_last = lax.fori_loop(0, nblk - 1, body, scores(0))
    r = lax.broadcasted_iota(jnp.int32, s_last.shape, 0)
    q_tok = qb * tq + (r & (tq - 1))
    k_idx = (nblk - 1) * KV_BLOCK + lax.broadcasted_iota(jnp.int32, s_last.shape, 1)
    update(nblk - 1, jnp.where(k_idx < ((q_tok // CHUNK) + 1) * CHUNK, s_last, MASK_VALUE))
    l_tot = jnp.broadcast_to(jnp.sum(l_ref[...], -1, keepdims=True), l_ref.shape)
    o = acc_ref[...] * jnp.tile(1.0 / l_tot, (1, MLA_RANK // LANES))
    _uv_project(o, wuv_ref, o_ref, tq)


def _attn_prompt(q, k, lw, batch, seq, tq):
    m = batch * seq
    nqc = seq // tq
    rows = MLA_HEADS * tq
    assert tq & (tq - 1) == 0 and tq % CHUNK == 0 and seq % tq == 0
    return pl.pallas_call(
        _attn_prompt_kernel, grid=(batch, nqc),
        in_specs=[pl.BlockSpec((MLA_HEADS, tq, QK_WIDTH), lambda b, c: (0, b * nqc + c, 0)),
                  pl.BlockSpec((seq, QK_WIDTH), lambda b, c: (b, 0)), lw.spec("w_uvp")],
        out_specs=pl.BlockSpec((tq, MLA_HEADS * MLA_D_V), lambda b, c: (b * nqc + c, 0)),
        out_shape=jax.ShapeDtypeStruct((m, MLA_HEADS * MLA_D_V), BF16),
        scratch_shapes=[pltpu.VMEM((rows, LANES), F32), pltpu.VMEM((rows, LANES), F32),
                        pltpu.VMEM((rows, MLA_RANK), F32)],
        compiler_params=_params("parallel", "arbitrary"), name="attn_prompt",
    )(q, k, lw["w_uvp"])


def _attn_sample_kernel(past_len, q_ref, k_ref, cckv_ref, ckrt_ref, wuv_ref, o_ref):
    nq = q_ref.shape[1]
    rows = MLA_HEADS * nq
    q = q_ref[...].reshape(rows, QK_WIDTH)
    q_lat = q[:, :MLA_RANK]
    q_rot = q[:, MLA_RANK:MLA_RANK + MLA_D_ROPE]
    nt = (((1,), (1,)), ((), ()))
    kc = cckv_ref[...].astype(BF16)
    s1 = (lax.dot_general(q_lat, kc, nt, preferred_element_type=F32)
          + jnp.dot(q_rot, ckrt_ref[...].astype(BF16), preferred_element_type=F32))
    kn = k_ref[...]
    s2 = lax.dot_general(q, kn, nt, preferred_element_type=F32)
    q_pos = past_len + lax.broadcasted_iota(jnp.int32, s2.shape, 0) % nq
    k_pos = past_len + lax.broadcasted_iota(jnp.int32, s2.shape, 1)
    s2 = jnp.where(k_pos // CHUNK <= q_pos // CHUNK, s2, MASK_VALUE)
    mx = jnp.maximum(jnp.max(s1, -1, keepdims=True), jnp.max(s2, -1, keepdims=True))
    p1 = jnp.exp2(s1 - mx)
    p2 = jnp.exp2(s2 - mx)
    l = jnp.sum(p1, -1, keepdims=True) + jnp.sum(p2, -1, keepdims=True)
    o = (jnp.dot(p1.astype(BF16), kc, preferred_element_type=F32)
         + jnp.dot(p2.astype(BF16), kn[:, :MLA_RANK], preferred_element_type=F32)) * (1.0 / l)
    _uv_project(o, wuv_ref, o_ref, nq)


def _attn_sample(q, k, cache_ckv, cache_krt, lw, batch, nq):
    past_len = cache_ckv.shape[2]
    layer = lw.layer
    return pl.pallas_call(
        functools.partial(_attn_sample_kernel, past_len), grid=(batch,),
        in_specs=[pl.BlockSpec((MLA_HEADS, nq, QK_WIDTH), lambda b: (0, b, 0)),
                  pl.BlockSpec((nq, QK_WIDTH), lambda b: (b, 0)),
                  pl.BlockSpec((None, None, past_len, MLA_RANK), lambda b: (layer, b, 0, 0)),
                  pl.BlockSpec((None, None, MLA_D_ROPE, past_len), lambda b: (layer, b, 0, 0)),
                  lw.spec("w_uvp")],
        out_specs=pl.BlockSpec((nq, MLA_HEADS * MLA_D_V), lambda b: (b, 0)),
        out_shape=jax.ShapeDtypeStruct((batch * nq, MLA_HEADS * MLA_D_V), BF16),
        compiler_params=_params("parallel"), name="attn_sample",
    )(q, k, cache_ckv, cache_krt, lw["w_uvp"])


def _hgrn_levels(lc):
    spans = []
    sp = lc // 2
    while sp >= 1:
        spans.append(sp)
        sp //= 2
    return spans


def _hgrn_consts(lc):
    spans = _hgrn_levels(lc)
    tri = np.tril(np.ones((lc, lc), np.float32))
    t = np.arange(lc)
    mats, masks = [tri], []
    for sp in spans:
        blk = t // (2 * sp)
        ref_row = blk * 2 * sp + sp - 1
        mats.append(np.abs(tri - tri[ref_row]))
        right = (t % (2 * sp)) >= sp
        masks.append(((blk[:, None] == blk[None, :]) & right[:, None] & ~right[None, :]).astype(np.float32))
    return np.tile(np.concatenate(mats, 0), (1, 3)), np.stack(masks, 0)


def _hgrn_kernel(has_init, q_ref, k_ref, lf_ref, v_ref, gt_ref, gn_ref, cmat_ref, mask_ref, *rest):
    s0_ref = rest[0] if has_init else None
    o_ref, sout_ref, st_ref = rest[-3:]
    c = pl.program_id(1)
    nb, lc, _ = q_ref.shape
    nlev = mask_ref.shape[0]
    nt = (((1,), (1,)), ((), ()))
    tn = (((0,), (0,)), ((), ()))

    @pl.when(c == 0)
    def _():
        for bi in range(nb):
            for h in range(HG_HEADS):
                if has_init:
                    st_ref[bi, h] = s0_ref[bi, h].T
                else:
                    st_ref[bi, h] = jnp.zeros((HG_DV, HG_DK), F32)

    eye = (lax.broadcasted_iota(jnp.int32, (lc, lc), 0) == lax.broadcasted_iota(jnp.int32, (lc, lc), 1))
    lvl_mask = [mask_ref[lv] > 0.5 for lv in range(nlev)]
    for bi in range(nb):
        lf = lf_ref[bi] * LOG2E
        lf_hi = lf.astype(BF16)
        r1 = lf - lf_hi.astype(F32)
        lf_mid = r1.astype(BF16)
        lf_lo = (r1 - lf_mid.astype(F32)).astype(BF16)
        br = jnp.dot(cmat_ref[...], jnp.concatenate([lf_hi, lf_mid, lf_lo], 0), preferred_element_type=F32)
        for h in range(HG_HEADS):
            sl = slice(h * HG_DK, (h + 1) * HG_DK)
            bh = br[:lc, sl]
            qh = q_ref[bi, :, sl]
            kh = k_ref[bi, :, sl]
            vh = v_ref[bi, :, sl]
            a = jnp.where(eye, jnp.sum(qh * kh, -1, keepdims=True), 0.0)
            for lv in range(nlev):
                e = jnp.exp2(br[(1 + lv) * lc:(2 + lv) * lc, sl])
                a = jnp.where(lvl_mask[lv],
                              lax.dot_general((qh * e).astype(BF16), (kh * e).astype(BF16), nt,
                                              preferred_element_type=F32), a)
            st = st_ref[bi, h]
            o = (jnp.dot(a.astype(BF16), vh, preferred_element_type=F32)
                 + lax.dot_general((qh * jnp.exp2(bh)).astype(BF16), st.astype(BF16), nt,
                                   preferred_element_type=F32))
            b_last = bh[lc - 1:lc, :]
            kdec = (kh * jnp.exp2(b_last - bh)).astype(BF16)
            st_ref[bi, h] = st * jnp.exp2(b_last) + lax.dot_general(vh, kdec, tn, preferred_element_type=F32)
            on = _rms_norm(o, gn_ref[:, sl])
            o_ref[bi, :, sl] = (on * gt_ref[bi, :, sl].astype(F32)).astype(BF16)

    @pl.when(c == pl.num_programs(1) - 1)
    def _():
        for bi in range(nb):
            for h in range(HG_HEADS):
                sout_ref[bi, h] = st_ref[bi, h].T


def _hgrn(hq, hk, hlf, hv, hgt, lw, state, batch, seq, lc, nb):
    nch = seq // lc
    assert batch % nb == 0
    cmat, masks = _hgrn_consts(lc)
    has_init = state is not None
    layer = lw.layer
    row = pl.BlockSpec((nb, lc, HG_W), lambda b, c: (b, c, 0))
    st_spec = pl.BlockSpec((nb, HG_HEADS, HG_DK, HG_DV), lambda b, c: (b, 0, 0, 0))
    in_specs = [row, row, row, row, row, lw.spec("hg_norm_g"), _const_spec(cmat.shape), _const_spec(masks.shape)]
    args = [t.reshape(batch, seq, HG_W) for t in (hq, hk, hlf, hv, hgt)]
    args += [lw["hg_norm_g"], jnp.asarray(cmat, BF16), jnp.asarray(masks)]
    if has_init:
        in_specs.append(pl.BlockSpec((None, nb, HG_HEADS, HG_DK, HG_DV), lambda b, c: (layer, b, 0, 0, 0)))
        args.append(state)
    o, s_fin = pl.pallas_call(
        functools.partial(_hgrn_kernel, has_init), grid=(batch // nb, nch),
        in_specs=in_specs, out_specs=[row, st_spec],
        out_shape=[jax.ShapeDtypeStruct((batch, seq, HG_W), BF16),
                   jax.ShapeDtypeStruct((batch, HG_HEADS, HG_DK, HG_DV), F32)],
        scratch_shapes=[pltpu.VMEM((nb, HG_HEADS, HG_DV, HG_DK), F32)],
        compiler_params=_params("parallel", "arbitrary"), name="hgrn",
    )(*args)
    return o.reshape(batch * seq, HG_W), s_fin


def _merge_kernel(alpha, cml, x_ref, ain_ref, hg_ref, u_ref, vcm_ref, sga_ref, sgb_ref, sgc_ref,
                  wpa_ref, wpb_ref, wpc_ref, wo_ref, ws_ref, bs_ref, g_ref, b_ref,
                  o_ref, cin_ref):
    tm = x_ref.shape[0]
    tril = (lax.broadcasted_iota(jnp.int32, (cml, cml), 0) >= lax.broadcasted_iota(jnp.int32, (cml, cml), 1))
    for g in range(CM_GROUPS):
        sl = slice(g * CM_GROUP_DIM, (g + 1) * CM_GROUP_DIM)
        wg = jnp.where(tril, ws_ref[g, :cml, :cml], 0.0).astype(BF16)
        for r in range(tm // cml):
            rs = slice(r * cml, (r + 1) * cml)
            s = jnp.dot(wg, vcm_ref[rs, sl].astype(BF16), preferred_element_type=F32) + bs_ref[:cml, sl]
            cin_ref[rs, sl] = (u_ref[rs, sl].astype(F32) * s).astype(BF16)
    y_a = jnp.dot(ain_ref[...], wpa_ref[...], preferred_element_type=F32)
    y_b = jnp.dot(hg_ref[...], wpb_ref[...], preferred_element_type=F32)
    y_c = jnp.dot(cin_ref[...], wpc_ref[...], preferred_element_type=F32)
    mrg = (sga_ref[...].astype(F32) * y_a + sgb_ref[...].astype(F32) * y_b
           + sgc_ref[...].astype(F32) * y_c).astype(BF16)
    y = alpha * x_ref[...] + jnp.dot(mrg, wo_ref[...], preferred_element_type=F32)
    o_ref[...] = _layer_norm(y, g_ref[...], b_ref[...])


def _merge(x, a_in, hg_o, u, vcm, sga, sgb, sgc, lw, alpha, cml, tm):
    m, d = x.shape
    row = lambda w: pl.BlockSpec((tm, w), lambda i: (i, 0))
    in_specs = [row(d), row(MLA_HEADS * MLA_D_V), row(HG_W), row(CM_WIDTH), row(CM_WIDTH), row(d), row(d), row(d),
                lw.spec("w_pa"), lw.spec("w_pb"), lw.spec("w_pc"), lw.spec("w_o"),
                lw.spec("cm_ws"), lw.spec("cm_bs_full"), lw.spec("ln1_g"), lw.spec("ln1_b")]
    return pl.pallas_call(
        functools.partial(_merge_kernel, alpha, cml), grid=(m // tm,), in_specs=in_specs,
        out_specs=row(d), out_shape=jax.ShapeDtypeStruct((m, d), F32),
        scratch_shapes=[pltpu.VMEM((tm, CM_WIDTH), BF16)],
        compiler_params=_params("parallel"), name="merge",
    )(x, a_in, hg_o, u, vcm, sga, sgb, sgc, lw["w_pa"], lw["w_pb"], lw["w_pc"], lw["w_o"],
      lw["cm_ws"], lw["cm_bs_full"], lw["ln1_g"], lw["ln1_b"])


def _ffn_kernel(alpha, nsplit, x_ref, wup_ref, wdn_ref, g_ref, b_ref, o_ref):
    x = x_ref[...]
    xb = x.astype(BF16)
    dff = wdn_ref.shape[0]
    cw = dff // nsplit
    y = alpha * x
    for j in range(nsplit):
        gate = jnp.dot(xb, wup_ref[:, j * cw:(j + 1) * cw], preferred_element_type=F32)
        up = jnp.dot(xb, wup_ref[:, dff + j * cw:dff + (j + 1) * cw], preferred_element_type=F32)
        act = (_silu(gate) * up).astype(BF16)
        y = y + jnp.dot(act, wdn_ref[j * cw:(j + 1) * cw, :], preferred_element_type=F32)
    o_ref[...] = _layer_norm(y, g_ref[...], b_ref[...])


def _ffn(x, lw, alpha, tm):
    m, d = x.shape
    row = pl.BlockSpec((tm, d), lambda i: (i, 0))
    return pl.pallas_call(
        functools.partial(_ffn_kernel, alpha, 2), grid=(m // tm,),
        in_specs=[row, lw.spec("w_up"), lw.spec("w_down"), lw.spec("ln2_g"), lw.spec("ln2_b")],
        out_specs=row, out_shape=jax.ShapeDtypeStruct((m, d), F32),
        compiler_params=_params("parallel"), name="ffn",
    )(x, lw["w_up"], lw["w_down"], lw["ln2_g"], lw["ln2_b"])


def _rope_table(pos):
    half = MLA_D_ROPE // 2
    inv = 1.0 / (ROPE_THETA ** (jnp.arange(half, dtype=F32) / half))
    ang = pos.astype(F32)[:, None] * inv[None]
    cos, sin = jnp.cos(ang), jnp.sin(ang)
    z = jnp.zeros((pos.shape[0], LANES - MLA_D_ROPE), F32)
    return jnp.concatenate([cos, cos, z, -sin, sin, z], -1)


def _prep_weights(w_in, q_norm_g, w_uq, w_uk, kv_norm_g, w_uv, hg_lb, hg_norm_g, cm_ln_g, cm_ln_b,
                  cm_ws, cm_bs, w_pa, w_pb, w_pc, w_o, ln1_g, ln1_b, w_up, w_down, ln2_g, ln2_b):
    depth, d, _ = w_in.shape
    half = MLA_D_ROPE // 2
    o_kr = 2 * MLA_RANK
    assert w_in.shape[2] - o_kr - MLA_D_ROPE == _B_END
    w_a = w_in[:, :, :o_kr].astype(BF16)
    w_b = w_in[:, :, o_kr + MLA_D_ROPE:].astype(BF16)
    kr = w_in[:, :, o_kr:o_kr + MLA_D_ROPE]
    zk = jnp.zeros((depth, d, LANES - MLA_D_ROPE), F32)
    w_kr = jnp.concatenate([kr, zk, kr[..., half:], kr[..., :half], zk], -1).astype(BF16)

    dq = MLA_D_NOPE + MLA_D_ROPE
    uq = w_uq.reshape(depth, MLA_RANK, MLA_HEADS, dq)
    nope = uq[..., :MLA_D_NOPE]
    r1, r2 = uq[..., MLA_D_NOPE:MLA_D_NOPE + half], uq[..., MLA_D_NOPE + half:]
    zn = jnp.zeros((depth, MLA_RANK, MLA_HEADS, LANES - MLA_D_NOPE), F32)
    zr = jnp.zeros((depth, MLA_RANK, MLA_HEADS, LANES - MLA_D_ROPE), F32)
    flat = lambda t: t.reshape(depth, MLA_RANK, MLA_HEADS * LANES)
    w_uq_r = jnp.concatenate([flat(jnp.concatenate([nope, zn], -1)),
                              flat(jnp.concatenate([r1, r2, zr], -1)),
                              flat(jnp.concatenate([r2, r1, zr], -1))], -1).astype(BF16)
    ukt = jnp.transpose(w_uk, (0, 2, 3, 1))
    w_uk_r = jnp.concatenate([ukt, jnp.zeros((depth, MLA_HEADS, LANES - MLA_D_NOPE, MLA_RANK), F32)], 2).astype(BF16)
    uvt = jnp.transpose(w_uv, (0, 2, 1, 3)).astype(BF16)
    lane_half = (jnp.arange(2 * MLA_D_V) // MLA_D_V)[None, None, None, :]
    head_half = (jnp.arange(MLA_HEADS) % 2)[None, :, None, None]
    w_uvp = jnp.where(lane_half == head_half, jnp.concatenate([uvt, uvt], -1), jnp.zeros((), BF16))

    sm = jax.nn.softmax(hg_lb.astype(F32), axis=0)
    lb = jnp.concatenate([jnp.zeros_like(sm[:1]), jnp.cumsum(sm[1:], axis=0)], axis=0)
    lbp = jnp.stack([jnp.log(lb + LB_TINY), jnp.log1p(-lb), 1.0 - lb], 1)

    bs_full = jnp.repeat(jnp.transpose(cm_bs, (0, 2, 1)), CM_GROUP_DIM, axis=-1)

    r3 = lambda t: t.reshape(depth, 1, -1)
    return dict(w_a=w_a, w_kr=w_kr, w_b=w_b, w_uq=w_uq_r, w_uk=w_uk_r, w_uvp=w_uvp, lbp=lbp,
                q_norm_g=r3(q_norm_g), kv_norm_g=r3(kv_norm_g), hg_norm_g=r3(hg_norm_g),
                cm_ln_g=r3(cm_ln_g), cm_ln_b=r3(cm_ln_b), cm_ws=cm_ws, cm_bs_full=bs_full,
                w_pa=w_pa.astype(BF16), w_pb=w_pb.astype(BF16), w_pc=w_pc.astype(BF16), w_o=w_o.astype(BF16),
                ln1_g=r3(ln1_g), ln1_b=r3(ln1_b), w_up=w_up.astype(BF16), w_down=w_down.astype(BF16),
                ln2_g=r3(ln2_g), ln2_b=r3(ln2_b))


def _hgrn_rows(batch):
    for nb in (4, 2, 1):
        if batch % nb == 0:
            return nb


def _row_tile(m, want):
    tm = min(m, want)
    assert m % tm == 0, (m, tm)
    return tm


def _layer(x, lw, rope_tab, alpha, batch, seq, past, prev, tm):
    prompt = past is None
    outs = _proj(x, lw, rope_tab, tm, BF16 if prompt else F32, prev)
    q, k, ckv, kr, hq, hk, hlf, hv, hgt, u, vcm, sga, sgb, sgc = outs
    if prompt:
        a_in = _attn_prompt(q, k, lw, batch, seq, Q_BLOCK)
        hg_o, s_fin = _hgrn(hq, hk, hlf, hv, hgt, lw, None, batch, seq, CHUNK, _hgrn_rows(batch))
    else:
        a_in = _attn_sample(q, k, past[0], past[1], lw, batch, seq)
        hg_o, s_fin = _hgrn(hq, hk, hlf, hv, hgt, lw, past[2], batch, seq, seq, _hgrn_rows(batch))
    cml = min(seq, CM_CHUNK)
    x = _merge(x, a_in, hg_o, u, vcm, sga, sgb, sgc, lw, alpha, cml, tm)
    x = _ffn(x, lw, alpha, tm)
    return x, (ckv, kr), (s_fin, vcm)


def kernel(x_prompt, x_sample, cache_mla_ckv, cache_mla_krope, state_hgrn, emb_ln_g, emb_ln_b, w_in, q_norm_g, w_uq, w_uk, kv_norm_g, w_uv, hg_lb, hg_norm_g, cm_ln_g, cm_ln_b, cm_ws, cm_bs, w_pa, w_pb, w_pc, w_o, ln1_g, ln1_b, w_up, w_down, ln2_g, ln2_b):
    bp, sp, d = x_prompt.shape
    bs, ss, _ = x_sample.shape
    depth = w_in.shape[0]
    past_len = cache_mla_ckv.shape[2]
    assert sp % CM_CHUNK == 0 and sp % KV_BLOCK == 0 and ss <= CHUNK and ss % 16 == 0
    alpha = float((2 * depth) ** 0.25)

    wts = _prep_weights(w_in, q_norm_g, w_uq, w_uk, kv_norm_g, w_uv, hg_lb, hg_norm_g, cm_ln_g, cm_ln_b,
                        cm_ws, cm_bs, w_pa, w_pb, w_pc, w_o, ln1_g, ln1_b, w_up, w_down, ln2_g, ln2_b)
    tm_p = _row_tile(bp * sp, 256)
    tm_s = _row_tile(bs * ss, 256)
    assert (sp % tm_p == 0 or tm_p % sp == 0) and (ss % tm_s == 0 or tm_s % ss == 0)
    rope_p = jnp.tile(_rope_table(jnp.arange(sp, dtype=jnp.int32)), (max(1, tm_p // sp), 1))
    rope_s = jnp.tile(_rope_table(past_len + jnp.arange(ss, dtype=jnp.int32)), (max(1, tm_s // ss), 1))
    eg, eb = emb_ln_g.reshape(1, d), emb_ln_b.reshape(1, d)
    xp = _input_ln(x_prompt.reshape(bp * sp, d), eg, eb, tm_p)
    xs = _input_ln(x_sample.reshape(bs * ss, d), eg, eb, tm_s)
    past = (cache_mla_ckv, jnp.swapaxes(cache_mla_krope, -1, -2), state_hgrn)

    kv_p = kv_s = None
    st_p, st_s, v_s = [], [], []
    for l in range(depth):
        lw = _LayerWeights(wts, l)
        xp, kv_p, (s_p, _) = _layer(xp, lw, rope_p, alpha, bp, sp, None, kv_p, tm_p)
        xs, kv_s, (s_s, v) = _layer(xs, lw, rope_s, alpha, bs, ss, past, kv_s, tm_s)
        st_p.append(s_p)
        st_s.append(s_s)
        v_s.append(v.reshape(bs, ss, -1))
    return (xp.reshape(bp, sp, d), xs.reshape(bs, ss, d),
            kv_p[0].reshape(depth, bp, sp, -1), kv_p[1].reshape(depth, bp, sp, -1), jnp.stack(st_p),
            kv_s[0].reshape(depth, bs, ss, -1), kv_s[1].reshape(depth, bs, ss, -1), jnp.stack(st_s),
            jnp.stack(v_s))
```

```python
import functools

import numpy as np
import jax
import jax.numpy as jnp
from jax import lax
from jax.experimental import pallas as pl
from jax.experimental.pallas import tpu as pltpu

F32 = jnp.float32
BF16 = jnp.bfloat16

CHUNK = 64
MLA_HEADS = 8
MLA_D_NOPE = 64
MLA_D_ROPE = 32
MLA_D_V = 64
MLA_RANK = 256
MLA_SCALE = (MLA_D_NOPE + MLA_D_ROPE) ** -0.5
LOG2E = float(np.log2(np.e))
Q_SCALE = MLA_SCALE * LOG2E
ROPE_THETA = 10000.0
MASK_VALUE = -1e30
HG_HEADS = 4
HG_DK = 128
HG_DV = 128
HG_W = HG_HEADS * HG_DK
LB_TINY = 1e-30
CM_CHUNK = 128
CM_GROUPS = 4
CM_WIDTH = 512
CM_GROUP_DIM = CM_WIDTH // CM_GROUPS
EPS = 1e-5

LANES = 128
VMEM_LIMIT_BYTES = 56 * 1024 * 1024
QK_WIDTH = MLA_RANK + LANES
KV_BLOCK = 256
Q_BLOCK = 128

_A_CQ, _A_CKV, _A_END = 0, 256, 512
_B_HQ, _B_HF, _B_HI, _B_HG, _B_CU, _B_CV, _B_GA, _B_GB, _B_GC, _B_END = (
    0, 512, 1024, 1536, 2048, 2560, 3072, 4096, 5120, 6144)


def _const_spec(shape):
    nd = len(shape)
    return pl.BlockSpec(shape, lambda *_: (0,) * nd, pipeline_mode=pl.Buffered(1))


class _LayerWeights:
    def __init__(self, stacked, layer):
        self.stacked = stacked
        self.layer = layer

    def __getitem__(self, name):
        return self.stacked[name]

    def spec(self, name):
        shape = self.stacked[name].shape[1:]
        layer, nd = self.layer, len(shape)
        return pl.BlockSpec((None,) + shape, lambda *_: (layer,) + (0,) * nd, pipeline_mode=pl.Buffered(1))


def _params(*sem):
    return pltpu.CompilerParams(dimension_semantics=sem, vmem_limit_bytes=VMEM_LIMIT_BYTES)


def _layer_norm(x, g, b):
    mu = jnp.mean(x, -1, keepdims=True)
    xc = x - mu
    var = jnp.mean(xc * xc, -1, keepdims=True)
    return xc * lax.rsqrt(var + EPS) * g + b


def _rms_norm(x, g):
    return x * lax.rsqrt(jnp.mean(x * x, -1, keepdims=True) + EPS) * g


def _gelu(x):
    return 0.5 * x * (1.0 + lax.erf(x * np.float32(1.0 / np.sqrt(2.0))))


def _sigmoid(x):
    return 1.0 / (1.0 + jnp.exp(-x))


def _silu(x):
    return x * _sigmoid(x)


def _log_sigmoid(x):
    return jnp.minimum(x, 0.0) - jnp.log(1.0 + jnp.exp(-jnp.abs(x)))


def _proj_kernel(n_alias, pre_ln, x_ref, wa_ref, wkr_ref, wb_ref, wuq_ref, wuk_ref, qg_ref, kvg_ref, rope_ref,
                 lbp_ref, cmg_ref, cmb_ref, *rest):
    (q_ref, kv_ref, ckv_ref, kr_ref, hq_ref, hk_ref, hlf_ref, hv_ref, hgt_ref,
     u_ref, vcm_ref, sga_ref, sgb_ref, sgc_ref) = rest[n_alias + 2 * pre_ln:][:14]
    x = x_ref[...]
    if pre_ln:
        x = _layer_norm(x, rest[0][...], rest[1][...])
        rest[-1][...] = x
    xb = x.astype(BF16)

    def proj_a(lo, hi):
        return jnp.dot(xb, wa_ref[:, lo:hi], preferred_element_type=F32)

    def proj(lo, hi):
        return jnp.dot(xb, wb_ref[:, lo:hi], preferred_element_type=F32)

    cos_t = rope_ref[:, :LANES]
    sin_t = rope_ref[:, LANES:]

    cqn = _rms_norm(proj_a(_A_CQ, _A_CKV), qg_ref[...]).astype(BF16)
    q3 = jnp.dot(cqn, wuq_ref[...], preferred_element_type=F32)
    hw = MLA_HEADS * LANES
    for h in range(MLA_HEADS):
        nope = q3[:, h * LANES:(h + 1) * LANES].astype(BF16)
        lat = jnp.dot(nope, wuk_ref[h], preferred_element_type=F32) * Q_SCALE
        rot = (q3[:, hw + h * LANES:hw + (h + 1) * LANES] * cos_t
               + q3[:, 2 * hw + h * LANES:2 * hw + (h + 1) * LANES] * sin_t) * Q_SCALE
        q_ref[h, :, :MLA_RANK] = lat.astype(BF16)
        q_ref[h, :, MLA_RANK:] = rot.astype(BF16)

    ckv = _rms_norm(proj_a(_A_CKV, _A_END), kvg_ref[...])
    ckv_ref[...] = ckv
    kv_ref[:, :MLA_RANK] = ckv.astype(BF16)
    zk = jnp.dot(xb, wkr_ref[...], preferred_element_type=F32)
    krot = zk[:, :LANES] * cos_t + zk[:, LANES:] * sin_t
    kr_ref[...] = krot[:, :MLA_D_ROPE]
    kv_ref[:, MLA_RANK:] = krot.astype(BF16)

    hq_ref[...] = proj(_B_HQ, _B_HF)
    zf = proj(_B_HF, _B_HI)
    log_lb = lbp_ref[0:1, :]
    log_1m = lbp_ref[1:2, :]
    one_m = lbp_ref[2:3, :]
    c = log_1m + _log_sigmoid(zf)
    hlf_ref[...] = jnp.maximum(log_lb, c) + jnp.log(1.0 + jnp.exp(-jnp.abs(log_lb - c)))
    hk_ref[...] = one_m * _sigmoid(-zf)
    hv_ref[...] = proj(_B_HI, _B_HG).astype(BF16)
    hgt_ref[...] = _silu(proj(_B_HG, _B_CU)).astype(BF16)

    u_ref[...] = _gelu(proj(_B_CU, _B_CV)).astype(BF16)
    vcm_ref[...] = _layer_norm(_gelu(proj(_B_CV, _B_GA)), cmg_ref[...], cmb_ref[...]).astype(vcm_ref.dtype)

    sga_ref[...] = _sigmoid(proj(_B_GA, _B_GB)).astype(BF16)
    sgb_ref[...] = _sigmoid(proj(_B_GB, _B_GC)).astype(BF16)
    sgc_ref[...] = _sigmoid(proj(_B_GC, _B_END)).astype(BF16)


def _proj(x, lw, rope_tab, tm, vcm_dtype, prev, emb):
    m, d = x.shape
    depth = lw["w_b"].shape[0]
    layer = lw.layer
    row = lambda w: pl.BlockSpec((tm, w), lambda i: (i, 0))
    lrow = lambda w: pl.BlockSpec((None, tm, w), lambda i: (layer, i, 0))
    out_shape = [
        jax.ShapeDtypeStruct((MLA_HEADS, m, QK_WIDTH), BF16),
        jax.ShapeDtypeStruct((m, QK_WIDTH), BF16),
        jax.ShapeDtypeStruct((depth, m, MLA_RANK), F32),
        jax.ShapeDtypeStruct((depth, m, MLA_D_ROPE), F32),
        jax.ShapeDtypeStruct((m, HG_W), F32),
        jax.ShapeDtypeStruct((m, HG_W), F32),
        jax.ShapeDtypeStruct((m, HG_W), F32),
        jax.ShapeDtypeStruct((m, HG_W), BF16),
        jax.ShapeDtypeStruct((m, HG_W), BF16),
        jax.ShapeDtypeStruct((m, CM_WIDTH), BF16),
        jax.ShapeDtypeStruct((m, CM_WIDTH), vcm_dtype),
        jax.ShapeDtypeStruct((m, d), BF16),
        jax.ShapeDtypeStruct((m, d), BF16),
        jax.ShapeDtypeStruct((m, d), BF16),
    ]
    out_specs = [
        pl.BlockSpec((MLA_HEADS, tm, QK_WIDTH), lambda i: (0, i, 0)),
        row(QK_WIDTH), lrow(MLA_RANK), lrow(MLA_D_ROPE),
        row(HG_W), row(HG_W), row(HG_W), row(HG_W), row(HG_W),
        row(CM_WIDTH), row(CM_WIDTH), row(d), row(d), row(d),
    ]
    names = ("w_a", "w_kr", "w_b", "w_uq", "w_uk", "q_norm_g", "kv_norm_g")
    tail = ("lbp", "cm_ln_g", "cm_ln_b")
    in_specs = ([row(d)] + [lw.spec(n) for n in names]
                + [pl.BlockSpec((tm, 2 * LANES), lambda i: (i % (rope_tab.shape[0] // tm), 0))]
                + [lw.spec(n) for n in tail])
    args = [x] + [lw[n] for n in names] + [rope_tab] + [lw[n] for n in tail]
    if emb is not None:
        in_specs += [_const_spec((1, d))] * 2
        args += list(emb)
        out_shape.append(jax.ShapeDtypeStruct((m, d), F32))
        out_specs.append(row(d))
    aliases = {}
    if prev is not None:
        aliases = {len(args): 2, len(args) + 1: 3}
        in_specs += [pl.BlockSpec(memory_space=pl.ANY)] * 2
        args += list(prev)
    return pl.pallas_call(
        functools.partial(_proj_kernel, len(aliases), emb is not None), grid=(m // tm,), in_specs=in_specs,
        out_specs=out_specs, out_shape=out_shape, input_output_aliases=aliases,
        compiler_params=_params("parallel"), name="proj",
    )(*args)


def _uv_project(o, wuv_ref, o_ref, tq):
    ob = o.astype(BF16)
    for i in range(MLA_HEADS // 2):
        pair = (jnp.dot(ob[2 * i * tq:(2 * i + 1) * tq], wuv_ref[2 * i], preferred_element_type=F32)
                + jnp.dot(ob[(2 * i + 1) * tq:(2 * i + 2) * tq], wuv_ref[2 * i + 1], preferred_element_type=F32))
        o_ref[:, i * LANES:(i + 1) * LANES] = pair.astype(BF16)


def _attn_prompt_kernel(q_ref, k_ref, wuv_ref, o_ref, m_ref, l_ref, acc_ref):
    qb = pl.program_id(1)
    tq = q_ref.shape[1]
    rows = MLA_HEADS * tq
    q = q_ref[...].reshape(rows, QK_WIDTH)
    kv_len = (qb + 1) * tq
    nblk = (kv_len + KV_BLOCK - 1) // KV_BLOCK
    rep = KV_BLOCK // LANES

    def scores(j):
        start = pl.multiple_of(j * KV_BLOCK, KV_BLOCK)
        return lax.dot_general(q, k_ref[pl.ds(start, KV_BLOCK), :], (((1,), (1,)), ((), ())),
                               preferred_element_type=F32)

    def values(j):
        start = pl.multiple_of(j * KV_BLOCK, KV_BLOCK)
        return k_ref[pl.ds(start, KV_BLOCK), :MLA_RANK]

    def lane_sums(p):
        return sum(p[:, i * LANES:(i + 1) * LANES] for i in range(rep))

    def update(j, s):
        m_prev = m_ref[...]
        m_new = jnp.maximum(m_prev, jnp.max(s, -1, keepdims=True))
        alpha = jnp.exp2(m_prev - m_new)
        p = jnp.exp2(s - jnp.tile(m_new, (1, rep)))
        l_ref[...] = alpha * l_ref[...] + lane_sums(p)
        acc_ref[...] = jnp.tile(alpha, (1, MLA_RANK // LANES)) * acc_ref[...] + jnp.dot(
            p.astype(BF16), values(j), preferred_element_type=F32)
        m_ref[...] = m_new

    s_diag = scores(nblk - 1)
    s_first = scores(0)
    lane = lax.broadcasted_iota(jnp.int32, (CHUNK, KV_BLOCK), 1)
    visible = [lane < qb * tq + (c + 1) * CHUNK - (nblk - 1) * KV_BLOCK for c in range(tq // CHUNK)]
    s_diag = jnp.concatenate(
        [jnp.where(visible[c], s_diag[h * tq + c * CHUNK:h * tq + (c + 1) * CHUNK], MASK_VALUE)
         for h in range(MLA_HEADS) for c in range(tq // CHUNK)], 0)
    m_diag = jnp.broadcast_to(jnp.max(s_diag, -1, keepdims=True), m_ref.shape)
    p_diag = jnp.exp2(s_diag - jnp.tile(m_diag, (1, rep)))
    m_ref[...] = m_diag
    l_ref[...] = lane_sums(p_diag)
    acc_ref[...] = jnp.dot(p_diag.astype(BF16), values(nblk - 1), preferred_element_type=F32)

    def body(j, s):
        s_next = scores(j + 1)
        update(j, s)
        return s_next

    s_last = lax.fori_loop(0, nblk - 2, body, s_first)

    @pl.when(nblk >= 2)
    def _():
        update(nblk - 2, s_last)

    l_tot = jnp.broadcast_to(jnp.sum(l_ref[...], -1, keepdims=True), l_ref.shape)
    o = acc_ref[...] * jnp.tile(1.0 / l_tot, (1, MLA_RANK // LANES))
    _uv_project(o, wuv_ref, o_ref, tq)


def _attn_prompt(q, k, lw, batch, seq, tq):
    m = batch * seq
    nqc = seq // tq
    rows = MLA_HEADS * tq
    assert tq & (tq - 1) == 0 and tq % CHUNK == 0 and seq % tq == 0
    return pl.pallas_call(
        _attn_prompt_kernel, grid=(batch, nqc),
        in_specs=[pl.BlockSpec((MLA_HEADS, tq, QK_WIDTH), lambda b, c: (0, b * nqc + c, 0)),
                  pl.BlockSpec((seq, QK_WIDTH), lambda b, c: (b, 0)), lw.spec("w_uvp")],
        out_specs=pl.BlockSpec((tq, MLA_HEADS * MLA_D_V), lambda b, c: (b * nqc + c, 0)),
        out_shape=jax.ShapeDtypeStruct((m, MLA_HEADS * MLA_D_V), BF16),
        scratch_shapes=[pltpu.VMEM((rows, LANES), F32), pltpu.VMEM((rows, LANES), F32),
                        pltpu.VMEM((rows, MLA_RANK), F32)],
        compiler_params=_params("parallel", "arbitrary"), name="attn_prompt",
    )(q, k, lw["w_uvp"])


def _attn_sample_kernel(past_len, q_ref, k_ref, cckv_ref, ckrt_ref, wuv_ref, o_ref):
    nq = q_ref.shape[1]
    rows = MLA_HEADS * nq
    q = q_ref[...].reshape(rows, QK_WIDTH)
    q_lat = q[:, :MLA_RANK]
    q_rot = q[:, MLA_RANK:MLA_RANK + MLA_D_ROPE]
    nt = (((1,), (1,)), ((), ()))
    kc = cckv_ref[...].astype(BF16)
    s1 = (lax.dot_general(q_lat, kc, nt, preferred_element_type=F32)
          + jnp.dot(q_rot, ckrt_ref[...].astype(BF16), preferred_element_type=F32))
    kn = k_ref[...]
    s2 = lax.dot_general(q, kn, nt, preferred_element_type=F32)
    q_pos = past_len + lax.broadcasted_iota(jnp.int32, s2.shape, 0) % nq
    k_pos = past_len + lax.broadcasted_iota(jnp.int32, s2.shape, 1)
    s2 = jnp.where(k_pos // CHUNK <= q_pos // CHUNK, s2, MASK_VALUE)
    mx = jnp.maximum(jnp.max(s1, -1, keepdims=True), jnp.max(s2, -1, keepdims=True))
    p1 = jnp.exp2(s1 - mx)
    p2 = jnp.exp2(s2 - mx)
    l = jnp.sum(p1, -1, keepdims=True) + jnp.sum(p2, -1, keepdims=True)
    o = (jnp.dot(p1.astype(BF16), kc, preferred_element_type=F32)
         + jnp.dot(p2.astype(BF16), kn[:, :MLA_RANK], preferred_element_type=F32)) * (1.0 / l)
    _uv_project(o, wuv_ref, o_ref, nq)


def _attn_sample(q, k, cache_ckv, cache_krt, lw, batch, nq):
    past_len = cache_ckv.shape[2]
    layer = lw.layer
    return pl.pallas_call(
        functools.partial(_attn_sample_kernel, past_len), grid=(batch,),
        in_specs=[pl.BlockSpec((MLA_HEADS, nq, QK_WIDTH), lambda b: (0, b, 0)),
                  pl.BlockSpec((nq, QK_WIDTH), lambda b: (b, 0)),
                  pl.BlockSpec((None, None, past_len, MLA_RANK), lambda b: (layer, b, 0, 0)),
                  pl.BlockSpec((None, None, MLA_D_ROPE, past_len), lambda b: (layer, b, 0, 0)),
                  lw.spec("w_uvp")],
        out_specs=pl.BlockSpec((nq, MLA_HEADS * MLA_D_V), lambda b: (b, 0)),
        out_shape=jax.ShapeDtypeStruct((batch * nq, MLA_HEADS * MLA_D_V), BF16),
        compiler_params=_params("parallel"), name="attn_sample",
    )(q, k, cache_ckv, cache_krt, lw["w_uvp"])


def _hgrn_levels(lc):
    spans = []
    sp = lc // 2
    while sp >= 1:
        spans.append(sp)
        sp //= 2
    return spans


def _hgrn_consts(lc):
    spans = _hgrn_levels(lc)
    tri = np.tril(np.ones((lc, lc), np.float32))
    t = np.arange(lc)
    mats, masks = [tri], []
    for sp in spans:
        blk = t // (2 * sp)
        ref_row = blk * 2 * sp + sp - 1
        mats.append(np.abs(tri - tri[ref_row]))
        right = (t % (2 * sp)) >= sp
        masks.append(((blk[:, None] == blk[None, :]) & right[:, None] & ~right[None, :]).astype(np.float32))
    return np.tile(np.concatenate(mats, 0), (1, 3)), np.stack(masks, 0)


def _hgrn_kernel(has_init, q_ref, k_ref, lf_ref, v_ref, gt_ref, gn_ref, cmat_ref, mask_ref, *rest):
    s0_ref = rest[0] if has_init else None
    o_ref, sout_ref, st_ref = rest[-3:]
    c = pl.program_id(1)
    nb, lc, _ = q_ref.shape
    nlev = mask_ref.shape[0]
    nt = (((1,), (1,)), ((), ()))
    tn = (((0,), (0,)), ((), ()))

    @pl.when(c == 0)
    def _():
        for bi in range(nb):
            for h in range(HG_HEADS):
                if has_init:
                    st_ref[bi, h] = s0_ref[bi, h].T
                else:
                    st_ref[bi, h] = jnp.zeros((HG_DV, HG_DK), F32)

    eye = (lax.broadcasted_iota(jnp.int32, (lc, lc), 0) == lax.broadcasted_iota(jnp.int32, (lc, lc), 1))
    lvl_mask = [mask_ref[lv] > 0.5 for lv in range(nlev)]
    for bi in range(nb):
        lf = lf_ref[bi] * LOG2E
        lf_hi = lf.astype(BF16)
        r1 = lf - lf_hi.astype(F32)
        lf_mid = r1.astype(BF16)
        lf_lo = (r1 - lf_mid.astype(F32)).astype(BF16)
        br = jnp.dot(cmat_ref[...], jnp.concatenate([lf_hi, lf_mid, lf_lo], 0), preferred_element_type=F32)
        for h in range(HG_HEADS):
            sl = slice(h * HG_DK, (h + 1) * HG_DK)
            bh = br[:lc, sl]
            qh = q_ref[bi, :, sl]
            kh = k_ref[bi, :, sl]
            vh = v_ref[bi, :, sl]
            a = jnp.where(eye, jnp.sum(qh * kh, -1, keepdims=True), 0.0)
            for lv in range(nlev):
                e = jnp.exp2(br[(1 + lv) * lc:(2 + lv) * lc, sl])
                a = jnp.where(lvl_mask[lv],
                              lax.dot_general((qh * e).astype(BF16), (kh * e).astype(BF16), nt,
                                              preferred_element_type=F32), a)
            st = st_ref[bi, h]
            o = (jnp.dot(a.astype(BF16), vh, preferred_element_type=F32)
                 + lax.dot_general((qh * jnp.exp2(bh)).astype(BF16), st.astype(BF16), nt,
                                   preferred_element_type=F32))
            b_last = bh[lc - 1:lc, :]
            kdec = (kh * jnp.exp2(b_last - bh)).astype(BF16)
            st_ref[bi, h] = st * jnp.exp2(b_last) + lax.dot_general(vh, kdec, tn, preferred_element_type=F32)
            on = _rms_norm(o, gn_ref[:, sl])
            o_ref[bi, :, sl] = (on * gt_ref[bi, :, sl].astype(F32)).astype(BF16)

    @pl.when(c == pl.num_programs(1) - 1)
    def _():
        for bi in range(nb):
            for h in range(HG_HEADS):
                sout_ref[bi, h] = st_ref[bi, h].T


def _hgrn(hq, hk, hlf, hv, hgt, lw, state, batch, seq, lc, nb):
    nch = seq // lc
    assert batch % nb == 0
    cmat, masks = _hgrn_consts(lc)
    has_init = state is not None
    layer = lw.layer
    row = pl.BlockSpec((nb, lc, HG_W), lambda b, c: (b, c, 0))
    st_spec = pl.BlockSpec((nb, HG_HEADS, HG_DK, HG_DV), lambda b, c: (b, 0, 0, 0))
    in_specs = [row, row, row, row, row, lw.spec("hg_norm_g"), _const_spec(cmat.shape), _const_spec(masks.shape)]
    args = [t.reshape(batch, seq, HG_W) for t in (hq, hk, hlf, hv, hgt)]
    args += [lw["hg_norm_g"], jnp.asarray(cmat, BF16), jnp.asarray(masks)]
    if has_init:
        in_specs.append(pl.BlockSpec((None, nb, HG_HEADS, HG_DK, HG_DV), lambda b, c: (layer, b, 0, 0, 0)))
        args.append(state)
    o, s_fin = pl.pallas_call(
        functools.partial(_hgrn_kernel, has_init), grid=(batch // nb, nch),
        in_specs=in_specs, out_specs=[row, st_spec],
        out_shape=[jax.ShapeDtypeStruct((batch, seq, HG_W), BF16),
                   jax.ShapeDtypeStruct((batch, HG_HEADS, HG_DK, HG_DV), F32)],
        scratch_shapes=[pltpu.VMEM((nb, HG_HEADS, HG_DV, HG_DK), F32)],
        compiler_params=_params("parallel", "arbitrary"), name="hgrn",
    )(*args)
    return o.reshape(batch * seq, HG_W), s_fin


def _merge_kernel(alpha, cml, x_ref, ain_ref, hg_ref, u_ref, vcm_ref, sga_ref, sgb_ref, sgc_ref,
                  wpa_ref, wpb_ref, wpc_ref, wo_ref, ws_ref, bs_ref, g_ref, b_ref,
                  o_ref, cin_ref):
    tm = x_ref.shape[0]
    tril = (lax.broadcasted_iota(jnp.int32, (cml, cml), 0) >= lax.broadcasted_iota(jnp.int32, (cml, cml), 1))
    for g in range(CM_GROUPS):
        sl = slice(g * CM_GROUP_DIM, (g + 1) * CM_GROUP_DIM)
        wg = jnp.where(tril, ws_ref[g, :cml, :cml], 0.0).astype(BF16)
        for r in range(tm // cml):
            rs = slice(r * cml, (r + 1) * cml)
            s = jnp.dot(wg, vcm_ref[rs, sl].astype(BF16), preferred_element_type=F32) + bs_ref[:cml, sl]
            cin_ref[rs, sl] = (u_ref[rs, sl].astype(F32) * s).astype(BF16)
    y_a = jnp.dot(ain_ref[...], wpa_ref[...], preferred_element_type=F32)
    y_b = jnp.dot(hg_ref[...], wpb_ref[...], preferred_element_type=F32)
    y_c = jnp.dot(cin_ref[...], wpc_ref[...], preferred_element_type=F32)
    mrg = (sga_ref[...].astype(F32) * y_a + sgb_ref[...].astype(F32) * y_b
           + sgc_ref[...].astype(F32) * y_c).astype(BF16)
    y = alpha * x_ref[...] + jnp.dot(mrg, wo_ref[...], preferred_element_type=F32)
    o_ref[...] = _layer_norm(y, g_ref[...], b_ref[...])


def _merge(x, a_in, hg_o, u, vcm, sga, sgb, sgc, lw, alpha, cml, tm):
    m, d = x.shape
    row = lambda w: pl.BlockSpec((tm, w), lambda i: (i, 0))
    in_specs = [row(d), row(MLA_HEADS * MLA_D_V), row(HG_W), row(CM_WIDTH), row(CM_WIDTH), row(d), row(d), row(d),
                lw.spec("w_pa"), lw.spec("w_pb"), lw.spec("w_pc"), lw.spec("w_o"),
                lw.spec("cm_ws"), lw.spec("cm_bs_full"), lw.spec("ln1_g"), lw.spec("ln1_b")]
    return pl.pallas_call(
        functools.partial(_merge_kernel, alpha, cml), grid=(m // tm,), in_specs=in_specs,
        out_specs=row(d), out_shape=jax.ShapeDtypeStruct((m, d), F32),
        scratch_shapes=[pltpu.VMEM((tm, CM_WIDTH), BF16)],
        compiler_params=_params("parallel"), name="merge",
    )(x, a_in, hg_o, u, vcm, sga, sgb, sgc, lw["w_pa"], lw["w_pb"], lw["w_pc"], lw["w_o"],
      lw["cm_ws"], lw["cm_bs_full"], lw["ln1_g"], lw["ln1_b"])


def _ffn_kernel(alpha, nsplit, x_ref, wup_ref, wdn_ref, g_ref, b_ref, o_ref):
    x = x_ref[...]
    xb = x.astype(BF16)
    dff = wdn_ref.shape[0]
    cw = dff // nsplit
    y = alpha * x
    for j in range(nsplit):
        gate = jnp.dot(xb, wup_ref[:, j * cw:(j + 1) * cw], preferred_element_type=F32)
        up = jnp.dot(xb, wup_ref[:, dff + j * cw:dff + (j + 1) * cw], preferred_element_type=F32)
        act = (_silu(gate) * up).astype(BF16)
        y = y + jnp.dot(act, wdn_ref[j * cw:(j + 1) * cw, :], preferred_element_type=F32)
    o_ref[...] = _layer_norm(y, g_ref[...], b_ref[...])


def _ffn(x, lw, alpha, tm):
    m, d = x.shape
    row = pl.BlockSpec((tm, d), lambda i: (i, 0))
    return pl.pallas_call(
        functools.partial(_ffn_kernel, alpha, 2), grid=(m // tm,),
        in_specs=[row, lw.spec("w_up"), lw.spec("w_down"), lw.spec("ln2_g"), lw.spec("ln2_b")],
        out_specs=row, out_shape=jax.ShapeDtypeStruct((m, d), F32),
        compiler_params=_params("parallel"), name="ffn",
    )(x, lw["w_up"], lw["w_down"], lw["ln2_g"], lw["ln2_b"])


def _rope_table(pos):
    half = MLA_D_ROPE // 2
    inv = 1.0 / (ROPE_THETA ** (jnp.arange(half, dtype=F32) / half))
    ang = pos.astype(F32)[:, None] * inv[None]
    cos, sin = jnp.cos(ang), jnp.sin(ang)
    z = jnp.zeros((pos.shape[0], LANES - MLA_D_ROPE), F32)
    return jnp.concatenate([cos, cos, z, -sin, sin, z], -1)


def _prep_weights(w_in, q_norm_g, w_uq, w_uk, kv_norm_g, w_uv, hg_lb, hg_norm_g, cm_ln_g, cm_ln_b,
                  cm_ws, cm_bs, w_pa, w_pb, w_pc, w_o, ln1_g, ln1_b, w_up, w_down, ln2_g, ln2_b):
    depth, d, _ = w_in.shape
    half = MLA_D_ROPE // 2
    o_kr = 2 * MLA_RANK
    assert w_in.shape[2] - o_kr - MLA_D_ROPE == _B_END
    w_a = w_in[:, :, :o_kr].astype(BF16)
    w_b = w_in[:, :, o_kr + MLA_D_ROPE:].astype(BF16)
    kr = w_in[:, :, o_kr:o_kr + MLA_D_ROPE]
    zk = jnp.zeros((depth, d, LANES - MLA_D_ROPE), F32)
    w_kr = jnp.concatenate([kr, zk, kr[..., half:], kr[..., :half], zk], -1).astype(BF16)

    dq = MLA_D_NOPE + MLA_D_ROPE
    uq = w_uq.reshape(depth, MLA_RANK, MLA_HEADS, dq)
    nope = uq[..., :MLA_D_NOPE]
    r1, r2 = uq[..., MLA_D_NOPE:MLA_D_NOPE + half], uq[..., MLA_D_NOPE + half:]
    zn = jnp.zeros((depth, MLA_RANK, MLA_HEADS, LANES - MLA_D_NOPE), F32)
    zr = jnp.zeros((depth, MLA_RANK, MLA_HEADS, LANES - MLA_D_ROPE), F32)
    flat = lambda t: t.reshape(depth, MLA_RANK, MLA_HEADS * LANES)
    w_uq_r = jnp.concatenate([flat(jnp.concatenate([nope, zn], -1)),
                              flat(jnp.concatenate([r1, r2, zr], -1)),
                              flat(jnp.concatenate([r2, r1, zr], -1))], -1).astype(BF16)
    ukt = jnp.transpose(w_uk, (0, 2, 3, 1))
    w_uk_r = jnp.concatenate([ukt, jnp.zeros((depth, MLA_HEADS, LANES - MLA_D_NOPE, MLA_RANK), F32)], 2).astype(BF16)
    uvt = jnp.transpose(w_uv, (0, 2, 1, 3)).astype(BF16)
    lane_half = (jnp.arange(2 * MLA_D_V) // MLA_D_V)[None, None, None, :]
    head_half = (jnp.arange(MLA_HEADS) % 2)[None, :, None, None]
    w_uvp = jnp.where(lane_half == head_half, jnp.concatenate([uvt, uvt], -1), jnp.zeros((), BF16))

    sm = jax.nn.softmax(hg_lb.astype(F32), axis=0)
    lb = jnp.concatenate([jnp.zeros_like(sm[:1]), jnp.cumsum(sm[1:], axis=0)], axis=0)
    lbp = jnp.stack([jnp.log(lb + LB_TINY), jnp.log1p(-lb), 1.0 - lb], 1)

    bs_full = jnp.repeat(jnp.transpose(cm_bs, (0, 2, 1)), CM_GROUP_DIM, axis=-1)

    r3 = lambda t: t.reshape(depth, 1, -1)
    return dict(w_a=w_a, w_kr=w_kr, w_b=w_b, w_uq=w_uq_r, w_uk=w_uk_r, w_uvp=w_uvp, lbp=lbp,
                q_norm_g=r3(q_norm_g), kv_norm_g=r3(kv_norm_g), hg_norm_g=r3(hg_norm_g),
                cm_ln_g=r3(cm_ln_g), cm_ln_b=r3(cm_ln_b), cm_ws=cm_ws, cm_bs_full=bs_full,
                w_pa=w_pa.astype(BF16), w_pb=w_pb.astype(BF16), w_pc=w_pc.astype(BF16), w_o=w_o.astype(BF16),
                ln1_g=r3(ln1_g), ln1_b=r3(ln1_b), w_up=w_up.astype(BF16), w_down=w_down.astype(BF16),
                ln2_g=r3(ln2_g), ln2_b=r3(ln2_b))


def _hgrn_rows(batch):
    for nb in (8, 4, 2, 1):
        if batch % nb == 0:
            return nb


def _row_tile(m, want):
    tm = min(m, want)
    assert m % tm == 0, (m, tm)
    return tm


def _layer(x, lw, rope_tab, alpha, batch, seq, past, prev, emb, tm):
    prompt = past is None
    outs = _proj(x, lw, rope_tab, tm, BF16 if prompt else F32, prev, emb)
    if emb is not None:
        x = outs[-1]
    q, k, ckv, kr, hq, hk, hlf, hv, hgt, u, vcm, sga, sgb, sgc = outs[:14]
    if prompt:
        a_in = _attn_prompt(q, k, lw, batch, seq, Q_BLOCK)
        hg_o, s_fin = _hgrn(hq, hk, hlf, hv, hgt, lw, None, batch, seq, CHUNK, _hgrn_rows(batch))
    else:
        a_in = _attn_sample(q, k, past[0], past[1], lw, batch, seq)
        hg_o, s_fin = _hgrn(hq, hk, hlf, hv, hgt, lw, past[2], batch, seq, seq, _hgrn_rows(batch))
    cml = min(seq, CM_CHUNK)
    x = _merge(x, a_in, hg_o, u, vcm, sga, sgb, sgc, lw, alpha, cml, tm)
    x = _ffn(x, lw, alpha, tm)
    return x, (ckv, kr), (s_fin, vcm)


def kernel(x_prompt, x_sample, cache_mla_ckv, cache_mla_krope, state_hgrn, emb_ln_g, emb_ln_b, w_in, q_norm_g, w_uq, w_uk, kv_norm_g, w_uv, hg_lb, hg_norm_g, cm_ln_g, cm_ln_b, cm_ws, cm_bs, w_pa, w_pb, w_pc, w_o, ln1_g, ln1_b, w_up, w_down, ln2_g, ln2_b):
    bp, sp, d = x_prompt.shape
    bs, ss, _ = x_sample.shape
    depth = w_in.shape[0]
    past_len = cache_mla_ckv.shape[2]
    assert sp % CM_CHUNK == 0 and sp % KV_BLOCK == 0 and ss <= CHUNK and ss % 16 == 0
    alpha = float((2 * depth) ** 0.25)

    wts = _prep_weights(w_in, q_norm_g, w_uq, w_uk, kv_norm_g, w_uv, hg_lb, hg_norm_g, cm_ln_g, cm_ln_b,
                        cm_ws, cm_bs, w_pa, w_pb, w_pc, w_o, ln1_g, ln1_b, w_up, w_down, ln2_g, ln2_b)
    tm_p = _row_tile(bp * sp, 256)
    tm_s = _row_tile(bs * ss, 256)
    assert (sp % tm_p == 0 or tm_p % sp == 0) and (ss % tm_s == 0 or tm_s % ss == 0)
    rope_p = jnp.tile(_rope_table(jnp.arange(sp, dtype=jnp.int32)), (max(1, tm_p // sp), 1))
    rope_s = jnp.tile(_rope_table(past_len + jnp.arange(ss, dtype=jnp.int32)), (max(1, tm_s // ss), 1))
    emb = (emb_ln_g.reshape(1, d), emb_ln_b.reshape(1, d))
    xp = x_prompt.reshape(bp * sp, d)
    xs = x_sample.reshape(bs * ss, d)
    past = (cache_mla_ckv, jnp.swapaxes(cache_mla_krope, -1, -2), state_hgrn)

    kv_p = kv_s = None
    st_p, st_s, v_s = [], [], []
    for l in range(depth):
        lw = _LayerWeights(wts, l)
        xp, kv_p, (s_p, _) = _layer(xp, lw, rope_p, alpha, bp, sp, None, kv_p, emb if l == 0 else None, tm_p)
        xs, kv_s, (s_s, v) = _layer(xs, lw, rope_s, alpha, bs, ss, past, kv_s, emb if l == 0 else None, tm_s)
        st_p.append(s_p)
        st_s.append(s_s)
        v_s.append(v.reshape(bs, ss, -1))
    return (xp.reshape(bp, sp, d), xs.reshape(bs, ss, d),
            kv_p[0].reshape(depth, bp, sp, -1), kv_p[1].reshape(depth, bp, sp, -1), jnp.stack(st_p),
            kv_s[0].reshape(depth, bs, ss, -1), kv_s[1].reshape(depth, bs, ss, -1), jnp.stack(st_s),
            jnp.stack(v_s))
```

```python
import functools

import numpy as np
import jax
import jax.numpy as jnp
from jax import lax
from jax.experimental import pallas as pl
from jax.experimental.pallas import tpu as pltpu

F32 = jnp.float32
BF16 = jnp.bfloat16

CHUNK = 64
MLA_HEADS = 8
MLA_D_NOPE = 64
MLA_D_ROPE = 32
MLA_D_V = 64
MLA_RANK = 256
MLA_SCALE = (MLA_D_NOPE + MLA_D_ROPE) ** -0.5
LOG2E = float(np.log2(np.e))
Q_SCALE = MLA_SCALE * LOG2E
ROPE_THETA = 10000.0
MASK_VALUE = -1e30
HG_HEADS = 4
HG_DK = 128
HG_DV = 128
HG_W = HG_HEADS * HG_DK
LB_TINY = 1e-30
LOGF_SPLIT = 3
CM_CHUNK = 128
CM_GROUPS = 4
CM_WIDTH = 512
CM_GROUP_DIM = CM_WIDTH // CM_GROUPS
EPS = 1e-5

LANES = 128
VMEM_LIMIT_BYTES = 56 * 1024 * 1024
QK_WIDTH = MLA_RANK + LANES
KV_BLOCK = 256
Q_BLOCK = 256
ROW_TILE = 256
WIDE_ROW_TILE = 512

_A_CQ, _A_CKV, _A_END = 0, 256, 512
_B_HQ, _B_HF, _B_HI, _B_HG, _B_CU, _B_CV, _B_GA, _B_GB, _B_GC, _B_END = (
    0, 512, 1024, 1536, 2048, 2560, 3072, 4096, 5120, 6144)


def _const_spec(shape):
    nd = len(shape)
    return pl.BlockSpec(shape, lambda *_: (0,) * nd, pipeline_mode=pl.Buffered(1))


class _LayerWeights:
    def __init__(self, stacked, layer):
        self.stacked = stacked
        self.layer = layer

    def __getitem__(self, name):
        return self.stacked[name]

    def spec(self, name):
        shape = self.stacked[name].shape[1:]
        layer, nd = self.layer, len(shape)
        return pl.BlockSpec((None,) + shape, lambda *_: (layer,) + (0,) * nd, pipeline_mode=pl.Buffered(1))


def _params(*sem):
    return pltpu.CompilerParams(dimension_semantics=sem, vmem_limit_bytes=VMEM_LIMIT_BYTES)


def _layer_norm(x, g, b):
    mu = jnp.mean(x, -1, keepdims=True)
    xc = x - mu
    var = jnp.mean(xc * xc, -1, keepdims=True)
    return xc * lax.rsqrt(var + EPS) * g + b


def _rms_norm(x, g):
    return x * lax.rsqrt(jnp.mean(x * x, -1, keepdims=True) + EPS) * g


def _gelu(x):
    return 0.5 * x * (1.0 + lax.erf(x * np.float32(1.0 / np.sqrt(2.0))))


def _sigmoid(x):
    return 1.0 / (1.0 + jnp.exp(-x))


def _silu(x):
    return x * _sigmoid(x)


def _log_sigmoid(x):
    return jnp.minimum(x, 0.0) - jnp.log(1.0 + jnp.exp(-jnp.abs(x)))


def _proj_kernel(n_alias, pre_ln, x_ref, wa_ref, wkr_ref, wb_ref, wuq_ref, wuk_ref, qg_ref, kvg_ref, rope_ref,
                 lbp_ref, cmg_ref, cmb_ref, *rest):
    (q_ref, kv_ref, ckv_ref, kr_ref, hq_ref, hk_ref, hlf_ref, hv_ref, hgt_ref,
     u_ref, vcm_ref, sga_ref, sgb_ref, sgc_ref) = rest[n_alias + 2 * pre_ln:][:14]
    x = x_ref[...]
    if pre_ln:
        x = _layer_norm(x, rest[0][...], rest[1][...])
        rest[-1][...] = x
    xb = x.astype(BF16)

    def proj_a(lo, hi):
        return jnp.dot(xb, wa_ref[:, lo:hi], preferred_element_type=F32)

    def proj(lo, hi):
        return jnp.dot(xb, wb_ref[:, lo:hi], preferred_element_type=F32)

    cos_t = rope_ref[:, :LANES]
    sin_t = rope_ref[:, LANES:]

    cqn = _rms_norm(proj_a(_A_CQ, _A_CKV), qg_ref[...]).astype(BF16)
    q3 = jnp.dot(cqn, wuq_ref[...], preferred_element_type=F32)
    hw = MLA_HEADS * LANES
    for h in range(MLA_HEADS):
        nope = q3[:, h * LANES:(h + 1) * LANES].astype(BF16)
        lat = jnp.dot(nope, wuk_ref[h], preferred_element_type=F32) * Q_SCALE
        rot = (q3[:, hw + h * LANES:hw + (h + 1) * LANES] * cos_t
               + q3[:, 2 * hw + h * LANES:2 * hw + (h + 1) * LANES] * sin_t) * Q_SCALE
        q_ref[h, :, :MLA_RANK] = lat.astype(BF16)
        q_ref[h, :, MLA_RANK:] = rot.astype(BF16)

    ckv = _rms_norm(proj_a(_A_CKV, _A_END), kvg_ref[...])
    ckv_ref[...] = ckv
    kv_ref[:, :MLA_RANK] = ckv.astype(BF16)
    zk = jnp.dot(xb, wkr_ref[...], preferred_element_type=F32)
    krot = zk[:, :LANES] * cos_t + zk[:, LANES:] * sin_t
    kr_ref[...] = krot[:, :MLA_D_ROPE]
    kv_ref[:, MLA_RANK:] = krot.astype(BF16)

    hq_ref[...] = proj(_B_HQ, _B_HF)
    zf = proj(_B_HF, _B_HI)
    log_lb = lbp_ref[0:1, :]
    log_1m = lbp_ref[1:2, :]
    one_m = lbp_ref[2:3, :]
    c = log_1m + _log_sigmoid(zf)
    hlf_ref[...] = jnp.maximum(log_lb, c) + jnp.log(1.0 + jnp.exp(-jnp.abs(log_lb - c)))
    hk_ref[...] = one_m * _sigmoid(-zf)
    hv_ref[...] = proj(_B_HI, _B_HG).astype(BF16)
    hgt_ref[...] = _silu(proj(_B_HG, _B_CU)).astype(BF16)

    u_ref[...] = _gelu(proj(_B_CU, _B_CV)).astype(BF16)
    vcm_ref[...] = _layer_norm(_gelu(proj(_B_CV, _B_GA)), cmg_ref[...], cmb_ref[...]).astype(vcm_ref.dtype)

    sga_ref[...] = _sigmoid(proj(_B_GA, _B_GB)).astype(BF16)
    sgb_ref[...] = _sigmoid(proj(_B_GB, _B_GC)).astype(BF16)
    sgc_ref[...] = _sigmoid(proj(_B_GC, _B_END)).astype(BF16)


def _proj(x, lw, rope_tab, tm, vcm_dtype, prev, emb):
    m, d = x.shape
    depth = lw["w_b"].shape[0]
    layer = lw.layer
    row = lambda w: pl.BlockSpec((tm, w), lambda i: (i, 0))
    lrow = lambda w: pl.BlockSpec((None, tm, w), lambda i: (layer, i, 0))
    out_shape = [
        jax.ShapeDtypeStruct((MLA_HEADS, m, QK_WIDTH), BF16),
        jax.ShapeDtypeStruct((m, QK_WIDTH), BF16),
        jax.ShapeDtypeStruct((depth, m, MLA_RANK), F32),
        jax.ShapeDtypeStruct((depth, m, MLA_D_ROPE), F32),
        jax.ShapeDtypeStruct((m, HG_W), F32),
        jax.ShapeDtypeStruct((m, HG_W), F32),
        jax.ShapeDtypeStruct((m, HG_W), F32),
        jax.ShapeDtypeStruct((m, HG_W), BF16),
        jax.ShapeDtypeStruct((m, HG_W), BF16),
        jax.ShapeDtypeStruct((m, CM_WIDTH), BF16),
        jax.ShapeDtypeStruct((m, CM_WIDTH), vcm_dtype),
        jax.ShapeDtypeStruct((m, d), BF16),
        jax.ShapeDtypeStruct((m, d), BF16),
        jax.ShapeDtypeStruct((m, d), BF16),
    ]
    out_specs = [
        pl.BlockSpec((MLA_HEADS, tm, QK_WIDTH), lambda i: (0, i, 0)),
        row(QK_WIDTH), lrow(MLA_RANK), lrow(MLA_D_ROPE),
        row(HG_W), row(HG_W), row(HG_W), row(HG_W), row(HG_W),
        row(CM_WIDTH), row(CM_WIDTH), row(d), row(d), row(d),
    ]
    names = ("w_a", "w_kr", "w_b", "w_uq", "w_uk", "q_norm_g", "kv_norm_g")
    tail = ("lbp", "cm_ln_g", "cm_ln_b")
    in_specs = ([row(d)] + [lw.spec(n) for n in names]
                + [pl.BlockSpec((tm, 2 * LANES), lambda i: (i % (rope_tab.shape[0] // tm), 0))]
                + [lw.spec(n) for n in tail])
    args = [x] + [lw[n] for n in names] + [rope_tab] + [lw[n] for n in tail]
    if emb is not None:
        in_specs += [_const_spec((1, d))] * 2
        args += list(emb)
        out_shape.append(jax.ShapeDtypeStruct((m, d), F32))
        out_specs.append(row(d))
    aliases = {}
    if prev is not None:
        aliases = {len(args): 2, len(args) + 1: 3}
        in_specs += [pl.BlockSpec(memory_space=pl.ANY)] * 2
        args += list(prev)
    return pl.pallas_call(
        functools.partial(_proj_kernel, len(aliases), emb is not None), grid=(m // tm,), in_specs=in_specs,
        out_specs=out_specs, out_shape=out_shape, input_output_aliases=aliases,
        compiler_params=_params("parallel"), name="proj",
    )(*args)


def _uv_project(o, wuv_ref, o_ref, tq):
    ob = o.astype(BF16)
    for i in range(MLA_HEADS // 2):
        pair = (jnp.dot(ob[2 * i * tq:(2 * i + 1) * tq], wuv_ref[2 * i], preferred_element_type=F32)
                + jnp.dot(ob[(2 * i + 1) * tq:(2 * i + 2) * tq], wuv_ref[2 * i + 1], preferred_element_type=F32))
        o_ref[:, i * LANES:(i + 1) * LANES] = pair.astype(BF16)


def _attn_prompt_kernel(q_ref, k_ref, wuv_ref, o_ref, m_ref, l_ref, acc_ref):
    qb = pl.program_id(1)
    tq = q_ref.shape[1]
    rows = MLA_HEADS * tq
    q = q_ref[...].reshape(rows, QK_WIDTH)
    kv_len = (qb + 1) * tq
    nblk = (kv_len + KV_BLOCK - 1) // KV_BLOCK
    rep = KV_BLOCK // LANES

    def scores(j):
        start = pl.multiple_of(j * KV_BLOCK, KV_BLOCK)
        return lax.dot_general(q, k_ref[pl.ds(start, KV_BLOCK), :], (((1,), (1,)), ((), ())),
                               preferred_element_type=F32)

    def values(j):
        start = pl.multiple_of(j * KV_BLOCK, KV_BLOCK)
        return k_ref[pl.ds(start, KV_BLOCK), :MLA_RANK]

    def lane_sums(p):
        return sum(p[:, i * LANES:(i + 1) * LANES] for i in range(rep))

    def update(j, s):
        m_prev = m_ref[...]
        m_new = jnp.maximum(m_prev, jnp.max(s, -1, keepdims=True))
        alpha = jnp.exp2(m_prev - m_new)
        p = jnp.exp2(s - jnp.tile(m_new, (1, rep)))
        l_ref[...] = alpha * l_ref[...] + lane_sums(p)
        acc_ref[...] = jnp.tile(alpha, (1, MLA_RANK // LANES)) * acc_ref[...] + jnp.dot(
            p.astype(BF16), values(j), preferred_element_type=F32)
        m_ref[...] = m_new

    s_diag = scores(nblk - 1)
    s_first = scores(0)
    lane = lax.broadcasted_iota(jnp.int32, (CHUNK, KV_BLOCK), 1)
    visible = [lane < qb * tq + (c + 1) * CHUNK - (nblk - 1) * KV_BLOCK for c in range(tq // CHUNK)]
    s_diag = jnp.concatenate(
        [jnp.where(visible[c], s_diag[h * tq + c * CHUNK:h * tq + (c + 1) * CHUNK], MASK_VALUE)
         for h in range(MLA_HEADS) for c in range(tq // CHUNK)], 0)
    m_diag = jnp.broadcast_to(jnp.max(s_diag, -1, keepdims=True), m_ref.shape)
    p_diag = jnp.exp2(s_diag - jnp.tile(m_diag, (1, rep)))
    m_ref[...] = m_diag
    l_ref[...] = lane_sums(p_diag)
    acc_ref[...] = jnp.dot(p_diag.astype(BF16), values(nblk - 1), preferred_element_type=F32)

    def body(j, s):
        s_next = scores(j + 1)
        update(j, s)
        return s_next

    s_last = lax.fori_loop(0, nblk - 2, body, s_first)

    @pl.when(nblk >= 2)
    def _():
        update(nblk - 2, s_last)

    l_tot = jnp.broadcast_to(jnp.sum(l_ref[...], -1, keepdims=True), l_ref.shape)
    o = acc_ref[...] * jnp.tile(1.0 / l_tot, (1, MLA_RANK // LANES))
    _uv_project(o, wuv_ref, o_ref, tq)


def _attn_prompt(q, k, lw, batch, seq, tq):
    m = batch * seq
    nqc = seq // tq
    rows = MLA_HEADS * tq
    assert tq & (tq - 1) == 0 and tq % CHUNK == 0 and seq % tq == 0
    return pl.pallas_call(
        _attn_prompt_kernel, grid=(batch, nqc),
        in_specs=[pl.BlockSpec((MLA_HEADS, tq, QK_WIDTH), lambda b, c: (0, b * nqc + c, 0)),
                  pl.BlockSpec((seq, QK_WIDTH), lambda b, c: (b, 0)), lw.spec("w_uvp")],
        out_specs=pl.BlockSpec((tq, MLA_HEADS * MLA_D_V), lambda b, c: (b * nqc + c, 0)),
        out_shape=jax.ShapeDtypeStruct((m, MLA_HEADS * MLA_D_V), BF16),
        scratch_shapes=[pltpu.VMEM((rows, LANES), F32), pltpu.VMEM((rows, LANES), F32),
                        pltpu.VMEM((rows, MLA_RANK), F32)],
        compiler_params=_params("parallel", "arbitrary"), name="attn_prompt",
    )(q, k, lw["w_uvp"])


def _attn_sample_kernel(past_len, q_ref, k_ref, cckv_ref, ckrt_ref, wuv_ref, o_ref):
    nq = q_ref.shape[1]
    rows = MLA_HEADS * nq
    q = q_ref[...].reshape(rows, QK_WIDTH)
    q_lat = q[:, :MLA_RANK]
    q_rot = q[:, MLA_RANK:MLA_RANK + MLA_D_ROPE]
    nt = (((1,), (1,)), ((), ()))
    kc = cckv_ref[...].astype(BF16)
    s1 = (lax.dot_general(q_lat, kc, nt, preferred_element_type=F32)
          + jnp.dot(q_rot, ckrt_ref[...].astype(BF16), preferred_element_type=F32))
    kn = k_ref[...]
    s2 = lax.dot_general(q, kn, nt, preferred_element_type=F32)
    q_pos = past_len + lax.broadcasted_iota(jnp.int32, s2.shape, 0) % nq
    k_pos = past_len + lax.broadcasted_iota(jnp.int32, s2.shape, 1)
    s2 = jnp.where(k_pos // CHUNK <= q_pos // CHUNK, s2, MASK_VALUE)
    mx = jnp.maximum(jnp.max(s1, -1, keepdims=True), jnp.max(s2, -1, keepdims=True))
    p1 = jnp.exp2(s1 - mx)
    p2 = jnp.exp2(s2 - mx)
    l = jnp.sum(p1, -1, keepdims=True) + jnp.sum(p2, -1, keepdims=True)
    o = (jnp.dot(p1.astype(BF16), kc, preferred_element_type=F32)
         + jnp.dot(p2.astype(BF16), kn[:, :MLA_RANK], preferred_element_type=F32)) * (1.0 / l)
    _uv_project(o, wuv_ref, o_ref, nq)


def _attn_sample(q, k, cache_ckv, cache_krt, lw, batch, nq):
    past_len = cache_ckv.shape[2]
    layer = lw.layer
    return pl.pallas_call(
        functools.partial(_attn_sample_kernel, past_len), grid=(batch,),
        in_specs=[pl.BlockSpec((MLA_HEADS, nq, QK_WIDTH), lambda b: (0, b, 0)),
                  pl.BlockSpec((nq, QK_WIDTH), lambda b: (b, 0)),
                  pl.BlockSpec((None, None, past_len, MLA_RANK), lambda b: (layer, b, 0, 0)),
                  pl.BlockSpec((None, None, MLA_D_ROPE, past_len), lambda b: (layer, b, 0, 0)),
                  lw.spec("w_uvp")],
        out_specs=pl.BlockSpec((nq, MLA_HEADS * MLA_D_V), lambda b: (b, 0)),
        out_shape=jax.ShapeDtypeStruct((batch * nq, MLA_HEADS * MLA_D_V), BF16),
        compiler_params=_params("parallel"), name="attn_sample",
    )(q, k, cache_ckv, cache_krt, lw["w_uvp"])


def _hgrn_levels(lc):
    spans = []
    sp = lc // 2
    while sp >= 1:
        spans.append(sp)
        sp //= 2
    return spans


def _hgrn_consts(lc):
    spans = _hgrn_levels(lc)
    tri = np.tril(np.ones((lc, lc), np.float32))
    t = np.arange(lc)
    mats, masks = [tri], []
    for sp in spans:
        blk = t // (2 * sp)
        ref_row = blk * 2 * sp + sp - 1
        mats.append(np.abs(tri - tri[ref_row]))
        right = (t % (2 * sp)) >= sp
        masks.append(((blk[:, None] == blk[None, :]) & right[:, None] & ~right[None, :]).astype(np.float32))
    return np.tile(np.concatenate(mats, 0), (1, LOGF_SPLIT)), np.stack(masks, 0)


def _hgrn_kernel(has_init, q_ref, k_ref, lf_ref, v_ref, gt_ref, gn_ref, cmat_ref, mask_ref, *rest):
    s0_ref = rest[0] if has_init else None
    o_ref, sout_ref, st_ref = rest[-3:]
    c = pl.program_id(1)
    nb, lc, _ = q_ref.shape
    nlev = mask_ref.shape[0]
    nt = (((1,), (1,)), ((), ()))
    tn = (((0,), (0,)), ((), ()))

    @pl.when(c == 0)
    def _():
        for bi in range(nb):
            for h in range(HG_HEADS):
                if has_init:
                    st_ref[bi, h] = s0_ref[bi, h].T
                else:
                    st_ref[bi, h] = jnp.zeros((HG_DV, HG_DK), F32)

    eye = (lax.broadcasted_iota(jnp.int32, (lc, lc), 0) == lax.broadcasted_iota(jnp.int32, (lc, lc), 1))
    lvl_mask = [mask_ref[lv] > 0.5 for lv in range(nlev)]
    for bi in range(nb):
        lf = lf_ref[bi] * LOG2E
        parts, rem = [], lf
        for _ in range(LOGF_SPLIT):
            parts.append(rem.astype(BF16))
            rem = rem - parts[-1].astype(F32)
        br = jnp.dot(cmat_ref[...], jnp.concatenate(parts, 0), preferred_element_type=F32)
        for h in range(HG_HEADS):
            sl = slice(h * HG_DK, (h + 1) * HG_DK)
            bh = br[:lc, sl]
            qh = q_ref[bi, :, sl]
            kh = k_ref[bi, :, sl]
            vh = v_ref[bi, :, sl]
            a = jnp.where(eye, jnp.sum(qh * kh, -1, keepdims=True), 0.0)
            for lv in range(nlev):
                e = jnp.exp2(br[(1 + lv) * lc:(2 + lv) * lc, sl])
                a = jnp.where(lvl_mask[lv],
                              lax.dot_general((qh * e).astype(BF16), (kh * e).astype(BF16), nt,
                                              preferred_element_type=F32), a)
            st = st_ref[bi, h]
            o = (jnp.dot(a.astype(BF16), vh, preferred_element_type=F32)
                 + lax.dot_general((qh * jnp.exp2(bh)).astype(BF16), st.astype(BF16), nt,
                                   preferred_element_type=F32))
            b_last = bh[lc - 1:lc, :]
            kdec = (kh * jnp.exp2(b_last - bh)).astype(BF16)
            st_ref[bi, h] = st * jnp.exp2(b_last) + lax.dot_general(vh, kdec, tn, preferred_element_type=F32)
            on = _rms_norm(o, gn_ref[:, sl])
            o_ref[bi, :, sl] = (on * gt_ref[bi, :, sl].astype(F32)).astype(BF16)

    @pl.when(c == pl.num_programs(1) - 1)
    def _():
        for bi in range(nb):
            for h in range(HG_HEADS):
                sout_ref[bi, h] = st_ref[bi, h].T


def _hgrn(hq, hk, hlf, hv, hgt, lw, state, batch, seq, lc, nb):
    nch = seq // lc
    assert batch % nb == 0
    cmat, masks = _hgrn_consts(lc)
    has_init = state is not None
    layer = lw.layer
    row = pl.BlockSpec((nb, lc, HG_W), lambda b, c: (b, c, 0))
    st_spec = pl.BlockSpec((nb, HG_HEADS, HG_DK, HG_DV), lambda b, c: (b, 0, 0, 0))
    in_specs = [row, row, row, row, row, lw.spec("hg_norm_g"), _const_spec(cmat.shape), _const_spec(masks.shape)]
    args = [t.reshape(batch, seq, HG_W) for t in (hq, hk, hlf, hv, hgt)]
    args += [lw["hg_norm_g"], jnp.asarray(cmat, BF16), jnp.asarray(masks)]
    if has_init:
        in_specs.append(pl.BlockSpec((None, nb, HG_HEADS, HG_DK, HG_DV), lambda b, c: (layer, b, 0, 0, 0)))
        args.append(state)
    o, s_fin = pl.pallas_call(
        functools.partial(_hgrn_kernel, has_init), grid=(batch // nb, nch),
        in_specs=in_specs, out_specs=[row, st_spec],
        out_shape=[jax.ShapeDtypeStruct((batch, seq, HG_W), BF16),
                   jax.ShapeDtypeStruct((batch, HG_HEADS, HG_DK, HG_DV), F32)],
        scratch_shapes=[pltpu.VMEM((nb, HG_HEADS, HG_DV, HG_DK), F32)],
        compiler_params=_params("parallel", "arbitrary"), name="hgrn",
    )(*args)
    return o.reshape(batch * seq, HG_W), s_fin


def _merge_kernel(alpha, cml, x_ref, ain_ref, hg_ref, u_ref, vcm_ref, sga_ref, sgb_ref, sgc_ref,
                  wpa_ref, wpb_ref, wpc_ref, wo_ref, ws_ref, bs_ref, g_ref, b_ref,
                  o_ref, cin_ref):
    tm = x_ref.shape[0]
    tril = (lax.broadcasted_iota(jnp.int32, (cml, cml), 0) >= lax.broadcasted_iota(jnp.int32, (cml, cml), 1))
    for g in range(CM_GROUPS):
        sl = slice(g * CM_GROUP_DIM, (g + 1) * CM_GROUP_DIM)
        wg = jnp.where(tril, ws_ref[g, :cml, :cml], 0.0).astype(BF16)
        for r in range(tm // cml):
            rs = slice(r * cml, (r + 1) * cml)
            s = jnp.dot(wg, vcm_ref[rs, sl].astype(BF16), preferred_element_type=F32) + bs_ref[:cml, sl]
            cin_ref[rs, sl] = (u_ref[rs, sl].astype(F32) * s).astype(BF16)
    y_a = jnp.dot(ain_ref[...], wpa_ref[...], preferred_element_type=F32)
    y_b = jnp.dot(hg_ref[...], wpb_ref[...], preferred_element_type=F32)
    y_c = jnp.dot(cin_ref[...], wpc_ref[...], preferred_element_type=F32)
    mrg = (sga_ref[...].astype(F32) * y_a + sgb_ref[...].astype(F32) * y_b
           + sgc_ref[...].astype(F32) * y_c).astype(BF16)
    y = alpha * x_ref[...] + jnp.dot(mrg, wo_ref[...], preferred_element_type=F32)
    o_ref[...] = _layer_norm(y, g_ref[...], b_ref[...])


def _merge(x, a_in, hg_o, u, vcm, sga, sgb, sgc, lw, alpha, cml, tm):
    m, d = x.shape
    row = lambda w: pl.BlockSpec((tm, w), lambda i: (i, 0))
    in_specs = [row(d), row(MLA_HEADS * MLA_D_V), row(HG_W), row(CM_WIDTH), row(CM_WIDTH), row(d), row(d), row(d),
                lw.spec("w_pa"), lw.spec("w_pb"), lw.spec("w_pc"), lw.spec("w_o"),
                lw.spec("cm_ws"), lw.spec("cm_bs_full"), lw.spec("ln1_g"), lw.spec("ln1_b")]
    return pl.pallas_call(
        functools.partial(_merge_kernel, alpha, cml), grid=(m // tm,), in_specs=in_specs,
        out_specs=row(d), out_shape=jax.ShapeDtypeStruct((m, d), F32),
        scratch_shapes=[pltpu.VMEM((tm, CM_WIDTH), BF16)],
        compiler_params=_params("parallel"), name="merge",
    )(x, a_in, hg_o, u, vcm, sga, sgb, sgc, lw["w_pa"], lw["w_pb"], lw["w_pc"], lw["w_o"],
      lw["cm_ws"], lw["cm_bs_full"], lw["ln1_g"], lw["ln1_b"])


def _ffn_kernel(alpha, nsplit, x_ref, wup_ref, wdn_ref, g_ref, b_ref, o_ref):
    x = x_ref[...]
    xb = x.astype(BF16)
    dff = wdn_ref.shape[0]
    cw = dff // nsplit
    y = alpha * x
    for j in range(nsplit):
        gate = jnp.dot(xb, wup_ref[:, j * cw:(j + 1) * cw], preferred_element_type=F32)
        up = jnp.dot(xb, wup_ref[:, dff + j * cw:dff + (j + 1) * cw], preferred_element_type=F32)
        act = (_silu(gate) * up).astype(BF16)
        y = y + jnp.dot(act, wdn_ref[j * cw:(j + 1) * cw, :], preferred_element_type=F32)
    o_ref[...] = _layer_norm(y, g_ref[...], b_ref[...])


def _ffn(x, lw, alpha, tm):
    m, d = x.shape
    row = pl.BlockSpec((tm, d), lambda i: (i, 0))
    return pl.pallas_call(
        functools.partial(_ffn_kernel, alpha, 2), grid=(m // tm,),
        in_specs=[row, lw.spec("w_up"), lw.spec("w_down"), lw.spec("ln2_g"), lw.spec("ln2_b")],
        out_specs=row, out_shape=jax.ShapeDtypeStruct((m, d), F32),
        compiler_params=_params("parallel"), name="ffn",
    )(x, lw["w_up"], lw["w_down"], lw["ln2_g"], lw["ln2_b"])


def _rope_table(pos):
    half = MLA_D_ROPE // 2
    inv = 1.0 / (ROPE_THETA ** (jnp.arange(half, dtype=F32) / half))
    ang = pos.astype(F32)[:, None] * inv[None]
    cos, sin = jnp.cos(ang), jnp.sin(ang)
    z = jnp.zeros((pos.shape[0], LANES - MLA_D_ROPE), F32)
    return jnp.concatenate([cos, cos, z, -sin, sin, z], -1)


def _prep_weights(w_in, q_norm_g, w_uq, w_uk, kv_norm_g, w_uv, hg_lb, hg_norm_g, cm_ln_g, cm_ln_b,
                  cm_ws, cm_bs, w_pa, w_pb, w_pc, w_o, ln1_g, ln1_b, w_up, w_down, ln2_g, ln2_b):
    depth, d, _ = w_in.shape
    half = MLA_D_ROPE // 2
    o_kr = 2 * MLA_RANK
    assert w_in.shape[2] - o_kr - MLA_D_ROPE == _B_END
    w_a = w_in[:, :, :o_kr].astype(BF16)
    w_b = w_in[:, :, o_kr + MLA_D_ROPE:].astype(BF16)
    kr = w_in[:, :, o_kr:o_kr + MLA_D_ROPE]
    zk = jnp.zeros((depth, d, LANES - MLA_D_ROPE), F32)
    w_kr = jnp.concatenate([kr, zk, kr[..., half:], kr[..., :half], zk], -1).astype(BF16)

    dq = MLA_D_NOPE + MLA_D_ROPE
    uq = w_uq.reshape(depth, MLA_RANK, MLA_HEADS, dq)
    nope = uq[..., :MLA_D_NOPE]
    r1, r2 = uq[..., MLA_D_NOPE:MLA_D_NOPE + half], uq[..., MLA_D_NOPE + half:]
    zn = jnp.zeros((depth, MLA_RANK, MLA_HEADS, LANES - MLA_D_NOPE), F32)
    zr = jnp.zeros((depth, MLA_RANK, MLA_HEADS, LANES - MLA_D_ROPE), F32)
    flat = lambda t: t.reshape(depth, MLA_RANK, MLA_HEADS * LANES)
    w_uq_r = jnp.concatenate([flat(jnp.concatenate([nope, zn], -1)),
                              flat(jnp.concatenate([r1, r2, zr], -1)),
                              flat(jnp.concatenate([r2, r1, zr], -1))], -1).astype(BF16)
    ukt = jnp.transpose(w_uk, (0, 2, 3, 1))
    w_uk_r = jnp.concatenate([ukt, jnp.zeros((depth, MLA_HEADS, LANES - MLA_D_NOPE, MLA_RANK), F32)], 2).astype(BF16)
    uvt = jnp.transpose(w_uv, (0, 2, 1, 3)).astype(BF16)
    lane_half = (jnp.arange(2 * MLA_D_V) // MLA_D_V)[None, None, None, :]
    head_half = (jnp.arange(MLA_HEADS) % 2)[None, :, None, None]
    w_uvp = jnp.where(lane_half == head_half, jnp.concatenate([uvt, uvt], -1), jnp.zeros((), BF16))

    sm = jax.nn.softmax(hg_lb.astype(F32), axis=0)
    lb = jnp.concatenate([jnp.zeros_like(sm[:1]), jnp.cumsum(sm[1:], axis=0)], axis=0)
    lbp = jnp.stack([jnp.log(lb + LB_TINY), jnp.log1p(-lb), 1.0 - lb], 1)

    bs_full = jnp.repeat(jnp.transpose(cm_bs, (0, 2, 1)), CM_GROUP_DIM, axis=-1)

    r3 = lambda t: t.reshape(depth, 1, -1)
    return dict(w_a=w_a, w_kr=w_kr, w_b=w_b, w_uq=w_uq_r, w_uk=w_uk_r, w_uvp=w_uvp, lbp=lbp,
                q_norm_g=r3(q_norm_g), kv_norm_g=r3(kv_norm_g), hg_norm_g=r3(hg_norm_g),
                cm_ln_g=r3(cm_ln_g), cm_ln_b=r3(cm_ln_b), cm_ws=cm_ws, cm_bs_full=bs_full,
                w_pa=w_pa.astype(BF16), w_pb=w_pb.astype(BF16), w_pc=w_pc.astype(BF16), w_o=w_o.astype(BF16),
                ln1_g=r3(ln1_g), ln1_b=r3(ln1_b), w_up=w_up.astype(BF16), w_down=w_down.astype(BF16),
                ln2_g=r3(ln2_g), ln2_b=r3(ln2_b))


def _hgrn_rows(batch):
    for nb in (8, 4, 2, 1):
        if batch % nb == 0:
            return nb


def _row_tile(m, want):
    tm = min(m, want)
    assert m % tm == 0, (m, tm)
    return tm


def _layer(x, lw, rope_tab, alpha, batch, seq, past, prev, emb, tm):
    prompt = past is None
    outs = _proj(x, lw, rope_tab, tm, BF16 if prompt else F32, prev, emb)
    if emb is not None:
        x = outs[-1]
    q, k, ckv, kr, hq, hk, hlf, hv, hgt, u, vcm, sga, sgb, sgc = outs[:14]
    if prompt:
        a_in = _attn_prompt(q, k, lw, batch, seq, Q_BLOCK)
        hg_o, s_fin = _hgrn(hq, hk, hlf, hv, hgt, lw, None, batch, seq, CHUNK, _hgrn_rows(batch))
    else:
        a_in = _attn_sample(q, k, past[0], past[1], lw, batch, seq)
        hg_o, s_fin = _hgrn(hq, hk, hlf, hv, hgt, lw, past[2], batch, seq, seq, _hgrn_rows(batch))
    cml = min(seq, CM_CHUNK)
    tm_wide = _row_tile(x.shape[0], WIDE_ROW_TILE)
    x = _merge(x, a_in, hg_o, u, vcm, sga, sgb, sgc, lw, alpha, cml, tm_wide)
    x = _ffn(x, lw, alpha, tm_wide)
    return x, (ckv, kr), (s_fin, vcm)


def kernel(x_prompt, x_sample, cache_mla_ckv, cache_mla_krope, state_hgrn, emb_ln_g, emb_ln_b, w_in, q_norm_g, w_uq, w_uk, kv_norm_g, w_uv, hg_lb, hg_norm_g, cm_ln_g, cm_ln_b, cm_ws, cm_bs, w_pa, w_pb, w_pc, w_o, ln1_g, ln1_b, w_up, w_down, ln2_g, ln2_b):
    bp, sp, d = x_prompt.shape
    bs, ss, _ = x_sample.shape
    depth = w_in.shape[0]
    past_len = cache_mla_ckv.shape[2]
    assert sp % CM_CHUNK == 0 and sp % KV_BLOCK == 0 and ss <= CHUNK and ss % 16 == 0
    alpha = float((2 * depth) ** 0.25)

    wts = _prep_weights(w_in, q_norm_g, w_uq, w_uk, kv_norm_g, w_uv, hg_lb, hg_norm_g, cm_ln_g, cm_ln_b,
                        cm_ws, cm_bs, w_pa, w_pb, w_pc, w_o, ln1_g, ln1_b, w_up, w_down, ln2_g, ln2_b)
    tm_p = _row_tile(bp * sp, ROW_TILE)
    tm_s = _row_tile(bs * ss, ROW_TILE)
    assert (sp % tm_p == 0 or tm_p % sp == 0) and (ss % tm_s == 0 or tm_s % ss == 0)
    rope_p = jnp.tile(_rope_table(jnp.arange(sp, dtype=jnp.int32)), (max(1, tm_p // sp), 1))
    rope_s = jnp.tile(_rope_table(past_len + jnp.arange(ss, dtype=jnp.int32)), (max(1, tm_s // ss), 1))
    emb = (emb_ln_g.reshape(1, d), emb_ln_b.reshape(1, d))
    xp = x_prompt.reshape(bp * sp, d)
    xs = x_sample.reshape(bs * ss, d)
    past = (cache_mla_ckv, jnp.swapaxes(cache_mla_krope, -1, -2), state_hgrn)

    kv_p = kv_s = None
    st_p, st_s, v_s = [], [], []
    for l in range(depth):
        lw = _LayerWeights(wts, l)
        xp, kv_p, (s_p, _) = _layer(xp, lw, rope_p, alpha, bp, sp, None, kv_p, emb if l == 0 else None, tm_p)
        xs, kv_s, (s_s, v) = _layer(xs, lw, rope_s, alpha, bs, ss, past, kv_s, emb if l == 0 else None, tm_s)
        st_p.append(s_p)
        st_s.append(s_s)
        v_s.append(v.reshape(bs, ss, -1))
    return (xp.reshape(bp, sp, d), xs.reshape(bs, ss, d),
            kv_p[0].reshape(depth, bp, sp, -1), kv_p[1].reshape(depth, bp, sp, -1), jnp.stack(st_p),
            kv_s[0].reshape(depth, bs, ss, -1), kv_s[1].reshape(depth, bs, ss, -1), jnp.stack(st_s),
            jnp.stack(v_s))
```

```python
import functools

import numpy as np
import jax
import jax.numpy as jnp
from jax import lax
from jax.experimental import pallas as pl
from jax.experimental.pallas import tpu as pltpu

F32 = jnp.float32
BF16 = jnp.bfloat16

CHUNK = 64
MLA_HEADS = 8
MLA_D_NOPE = 64
MLA_D_ROPE = 32
MLA_D_V = 64
MLA_RANK = 256
MLA_SCALE = (MLA_D_NOPE + MLA_D_ROPE) ** -0.5
LOG2E = float(np.log2(np.e))
Q_SCALE = MLA_SCALE * LOG2E
ROPE_THETA = 10000.0
MASK_VALUE = -1e30
HG_HEADS = 4
HG_DK = 128
HG_DV = 128
HG_W = HG_HEADS * HG_DK
LB_TINY = 1e-30
LOGF_SPLIT = 3
CM_CHUNK = 128
CM_GROUPS = 4
CM_WIDTH = 512
CM_GROUP_DIM = CM_WIDTH // CM_GROUPS
EPS = 1e-5

LANES = 128
VMEM_LIMIT_BYTES = 56 * 1024 * 1024
QK_WIDTH = MLA_RANK + LANES
KV_BLOCK = 256
Q_BLOCK = 256
ROW_TILE = 256
WIDE_ROW_TILE = 512

_A_CQ, _A_CKV, _A_END = 0, 256, 512
_B_HQ, _B_HF, _B_HI, _B_HG, _B_CU, _B_CV, _B_GA, _B_GB, _B_GC, _B_END = (
    0, 512, 1024, 1536, 2048, 2560, 3072, 4096, 5120, 6144)


def _const_spec(shape):
    nd = len(shape)
    return pl.BlockSpec(shape, lambda *_: (0,) * nd, pipeline_mode=pl.Buffered(1))


class _LayerWeights:
    def __init__(self, stacked, layer):
        self.stacked = stacked
        self.layer = layer

    def __getitem__(self, name):
        return self.stacked[name]

    def spec(self, name):
        shape = self.stacked[name].shape[1:]
        layer, nd = self.layer, len(shape)
        return pl.BlockSpec((None,) + shape, lambda *_: (layer,) + (0,) * nd, pipeline_mode=pl.Buffered(1))


def _params(*sem):
    return pltpu.CompilerParams(dimension_semantics=sem, vmem_limit_bytes=VMEM_LIMIT_BYTES)


def _layer_norm(x, g, b):
    mu = jnp.mean(x, -1, keepdims=True)
    xc = x - mu
    var = jnp.mean(xc * xc, -1, keepdims=True)
    return xc * lax.rsqrt(var + EPS) * g + b


def _rms_norm(x, g):
    return x * lax.rsqrt(jnp.mean(x * x, -1, keepdims=True) + EPS) * g


def _gelu(x):
    return 0.5 * x * (1.0 + lax.erf(x * np.float32(1.0 / np.sqrt(2.0))))


def _sigmoid(x):
    return 1.0 / (1.0 + jnp.exp(-x))


def _silu(x):
    return x * _sigmoid(x)


def _log_sigmoid(x):
    return jnp.minimum(x, 0.0) - jnp.log(1.0 + jnp.exp(-jnp.abs(x)))


def _proj_kernel(n_alias, pre_ln, x_ref, wa_ref, wkr_ref, wb_ref, wuq_ref, wuk_ref, qg_ref, kvg_ref, rope_ref,
                 lbp_ref, cmg_ref, cmb_ref, *rest):
    (q_ref, kv_ref, ckv_ref, kr_ref, hq_ref, hk_ref, hlf_ref, hv_ref, hgt_ref,
     u_ref, vcm_ref, sga_ref, sgb_ref, sgc_ref) = rest[n_alias + 2 * pre_ln:][:14]
    x = x_ref[...]
    if pre_ln:
        x = _layer_norm(x, rest[0][...], rest[1][...])
        rest[-1][...] = x
    xb = x.astype(BF16)

    def proj_a(lo, hi):
        return jnp.dot(xb, wa_ref[:, lo:hi], preferred_element_type=F32)

    def proj(lo, hi):
        return jnp.dot(xb, wb_ref[:, lo:hi], preferred_element_type=F32)

    cos_t = rope_ref[:, :LANES]
    sin_t = rope_ref[:, LANES:]

    cqn = _rms_norm(proj_a(_A_CQ, _A_CKV), qg_ref[...]).astype(BF16)
    q3 = jnp.dot(cqn, wuq_ref[...], preferred_element_type=F32)
    hw = MLA_HEADS * LANES
    for h in range(MLA_HEADS):
        nope = q3[:, h * LANES:(h + 1) * LANES].astype(BF16)
        lat = jnp.dot(nope, wuk_ref[h], preferred_element_type=F32) * Q_SCALE
        rot = (q3[:, hw + h * LANES:hw + (h + 1) * LANES] * cos_t
               + q3[:, 2 * hw + h * LANES:2 * hw + (h + 1) * LANES] * sin_t) * Q_SCALE
        q_ref[h, :, :MLA_RANK] = lat.astype(BF16)
        q_ref[h, :, MLA_RANK:] = rot.astype(BF16)

    ckv = _rms_norm(proj_a(_A_CKV, _A_END), kvg_ref[...])
    ckv_ref[...] = ckv
    kv_ref[:, :MLA_RANK] = ckv.astype(BF16)
    zk = jnp.dot(xb, wkr_ref[...], preferred_element_type=F32)
    krot = zk[:, :LANES] * cos_t + zk[:, LANES:] * sin_t
    kr_ref[...] = krot[:, :MLA_D_ROPE]
    kv_ref[:, MLA_RANK:] = krot.astype(BF16)

    hq_ref[...] = proj(_B_HQ, _B_HF)
    zf = proj(_B_HF, _B_HI)
    log_lb = lbp_ref[0:1, :]
    log_1m = lbp_ref[1:2, :]
    one_m = lbp_ref[2:3, :]
    c = log_1m + _log_sigmoid(zf)
    hlf_ref[...] = jnp.maximum(log_lb, c) + jnp.log(1.0 + jnp.exp(-jnp.abs(log_lb - c)))
    hk_ref[...] = one_m * _sigmoid(-zf)
    hv_ref[...] = proj(_B_HI, _B_HG).astype(BF16)
    hgt_ref[...] = _silu(proj(_B_HG, _B_CU)).astype(BF16)

    u_ref[...] = _gelu(proj(_B_CU, _B_CV)).astype(BF16)
    vcm_ref[...] = _layer_norm(_gelu(proj(_B_CV, _B_GA)), cmg_ref[...], cmb_ref[...]).astype(vcm_ref.dtype)

    sga_ref[...] = _sigmoid(proj(_B_GA, _B_GB)).astype(BF16)
    sgb_ref[...] = _sigmoid(proj(_B_GB, _B_GC)).astype(BF16)
    sgc_ref[...] = _sigmoid(proj(_B_GC, _B_END)).astype(BF16)


def _proj(x, lw, rope_tab, tm, vcm_dtype, prev, emb):
    m, d = x.shape
    depth = lw["w_b"].shape[0]
    layer = lw.layer
    row = lambda w: pl.BlockSpec((tm, w), lambda i: (i, 0))
    lrow = lambda w: pl.BlockSpec((None, tm, w), lambda i: (layer, i, 0))
    out_shape = [
        jax.ShapeDtypeStruct((MLA_HEADS, m, QK_WIDTH), BF16),
        jax.ShapeDtypeStruct((m, QK_WIDTH), BF16),
        jax.ShapeDtypeStruct((depth, m, MLA_RANK), F32),
        jax.ShapeDtypeStruct((depth, m, MLA_D_ROPE), F32),
        jax.ShapeDtypeStruct((m, HG_W), F32),
        jax.ShapeDtypeStruct((m, HG_W), F32),
        jax.ShapeDtypeStruct((m, HG_W), F32),
        jax.ShapeDtypeStruct((m, HG_W), BF16),
        jax.ShapeDtypeStruct((m, HG_W), BF16),
        jax.ShapeDtypeStruct((m, CM_WIDTH), BF16),
        jax.ShapeDtypeStruct((m, CM_WIDTH), vcm_dtype),
        jax.ShapeDtypeStruct((m, d), BF16),
        jax.ShapeDtypeStruct((m, d), BF16),
        jax.ShapeDtypeStruct((m, d), BF16),
    ]
    out_specs = [
        pl.BlockSpec((MLA_HEADS, tm, QK_WIDTH), lambda i: (0, i, 0)),
        row(QK_WIDTH), lrow(MLA_RANK), lrow(MLA_D_ROPE),
        row(HG_W), row(HG_W), row(HG_W), row(HG_W), row(HG_W),
        row(CM_WIDTH), row(CM_WIDTH), row(d), row(d), row(d),
    ]
    names = ("w_a", "w_kr", "w_b", "w_uq", "w_uk", "q_norm_g", "kv_norm_g")
    tail = ("lbp", "cm_ln_g", "cm_ln_b")
    in_specs = ([row(d)] + [lw.spec(n) for n in names]
                + [pl.BlockSpec((tm, 2 * LANES), lambda i: (i % (rope_tab.shape[0] // tm), 0))]
                + [lw.spec(n) for n in tail])
    args = [x] + [lw[n] for n in names] + [rope_tab] + [lw[n] for n in tail]
    if emb is not None:
        in_specs += [_const_spec((1, d))] * 2
        args += list(emb)
        out_shape.append(jax.ShapeDtypeStruct((m, d), F32))
        out_specs.append(row(d))
    aliases = {len(args): 2, len(args) + 1: 3}
    in_specs += [pl.BlockSpec(memory_space=pl.ANY)] * 2
    args += list(prev)
    return pl.pallas_call(
        functools.partial(_proj_kernel, len(aliases), emb is not None), grid=(m // tm,), in_specs=in_specs,
        out_specs=out_specs, out_shape=out_shape, input_output_aliases=aliases,
        compiler_params=_params("parallel"), name="proj",
    )(*args)


def _uv_project(o, wuv_ref, o_ref, tq):
    ob = o.astype(BF16)
    for i in range(MLA_HEADS // 2):
        pair = (jnp.dot(ob[2 * i * tq:(2 * i + 1) * tq], wuv_ref[2 * i], preferred_element_type=F32)
                + jnp.dot(ob[(2 * i + 1) * tq:(2 * i + 2) * tq], wuv_ref[2 * i + 1], preferred_element_type=F32))
        o_ref[:, i * LANES:(i + 1) * LANES] = pair.astype(BF16)


def _attn_prompt_kernel(q_ref, k_ref, wuv_ref, o_ref, m_ref, l_ref, acc_ref):
    qb = pl.program_id(1)
    tq = q_ref.shape[1]
    rows = MLA_HEADS * tq
    q = q_ref[...].reshape(rows, QK_WIDTH)
    kv_len = (qb + 1) * tq
    nblk = (kv_len + KV_BLOCK - 1) // KV_BLOCK
    rep = KV_BLOCK // LANES

    def scores(j):
        start = pl.multiple_of(j * KV_BLOCK, KV_BLOCK)
        return lax.dot_general(q, k_ref[pl.ds(start, KV_BLOCK), :], (((1,), (1,)), ((), ())),
                               preferred_element_type=F32)

    def values(j):
        start = pl.multiple_of(j * KV_BLOCK, KV_BLOCK)
        return k_ref[pl.ds(start, KV_BLOCK), :MLA_RANK]

    def lane_sums(p):
        return sum(p[:, i * LANES:(i + 1) * LANES] for i in range(rep))

    def update(j, s):
        m_prev = m_ref[...]
        m_new = jnp.maximum(m_prev, jnp.max(s, -1, keepdims=True))
        alpha = jnp.exp2(m_prev - m_new)
        p = jnp.exp2(s - jnp.tile(m_new, (1, rep)))
        l_ref[...] = alpha * l_ref[...] + lane_sums(p)
        acc_ref[...] = jnp.tile(alpha, (1, MLA_RANK // LANES)) * acc_ref[...] + jnp.dot(
            p.astype(BF16), values(j), preferred_element_type=F32)
        m_ref[...] = m_new

    s_diag = scores(nblk - 1)
    s_first = scores(0)
    lane = lax.broadcasted_iota(jnp.int32, (CHUNK, KV_BLOCK), 1)
    visible = [lane < qb * tq + (c + 1) * CHUNK - (nblk - 1) * KV_BLOCK for c in range(tq // CHUNK)]
    s_diag = jnp.concatenate(
        [jnp.where(visible[c], s_diag[h * tq + c * CHUNK:h * tq + (c + 1) * CHUNK], MASK_VALUE)
         for h in range(MLA_HEADS) for c in range(tq // CHUNK)], 0)
    m_diag = jnp.broadcast_to(jnp.max(s_diag, -1, keepdims=True), m_ref.shape)
    p_diag = jnp.exp2(s_diag - jnp.tile(m_diag, (1, rep)))
    m_ref[...] = m_diag
    l_ref[...] = lane_sums(p_diag)
    acc_ref[...] = jnp.dot(p_diag.astype(BF16), values(nblk - 1), preferred_element_type=F32)

    def body(j, s):
        s_next = scores(j + 1)
        update(j, s)
        return s_next

    s_last = lax.fori_loop(0, nblk - 2, body, s_first)

    @pl.when(nblk >= 2)
    def _():
        update(nblk - 2, s_last)

    l_tot = jnp.broadcast_to(jnp.sum(l_ref[...], -1, keepdims=True), l_ref.shape)
    o = acc_ref[...] * jnp.tile(1.0 / l_tot, (1, MLA_RANK // LANES))
    _uv_project(o, wuv_ref, o_ref, tq)


def _attn_prompt(q, k, lw, batch, seq, tq):
    m = batch * seq
    nqc = seq // tq
    rows = MLA_HEADS * tq
    assert tq & (tq - 1) == 0 and tq % CHUNK == 0 and seq % tq == 0
    return pl.pallas_call(
        _attn_prompt_kernel, grid=(batch, nqc),
        in_specs=[pl.BlockSpec((MLA_HEADS, tq, QK_WIDTH), lambda b, c: (0, b * nqc + c, 0)),
                  pl.BlockSpec((seq, QK_WIDTH), lambda b, c: (b, 0)), lw.spec("w_uvp")],
        out_specs=pl.BlockSpec((tq, MLA_HEADS * MLA_D_V), lambda b, c: (b * nqc + c, 0)),
        out_shape=jax.ShapeDtypeStruct((m, MLA_HEADS * MLA_D_V), BF16),
        scratch_shapes=[pltpu.VMEM((rows, LANES), F32), pltpu.VMEM((rows, LANES), F32),
                        pltpu.VMEM((rows, MLA_RANK), F32)],
        compiler_params=_params("parallel", "arbitrary"), name="attn_prompt",
    )(q, k, lw["w_uvp"])


def _attn_sample_kernel(past_len, q_ref, k_ref, cckv_ref, ckrt_ref, wuv_ref, o_ref):
    nq = q_ref.shape[1]
    rows = MLA_HEADS * nq
    q = q_ref[...].reshape(rows, QK_WIDTH)
    q_lat = q[:, :MLA_RANK]
    q_rot = q[:, MLA_RANK:MLA_RANK + MLA_D_ROPE]
    nt = (((1,), (1,)), ((), ()))
    kc = cckv_ref[...].astype(BF16)
    s1 = (lax.dot_general(q_lat, kc, nt, preferred_element_type=F32)
          + jnp.dot(q_rot, ckrt_ref[...].astype(BF16), preferred_element_type=F32))
    kn = k_ref[...]
    s2 = lax.dot_general(q, kn, nt, preferred_element_type=F32)
    q_pos = past_len + lax.broadcasted_iota(jnp.int32, s2.shape, 0) % nq
    k_pos = past_len + lax.broadcasted_iota(jnp.int32, s2.shape, 1)
    s2 = jnp.where(k_pos // CHUNK <= q_pos // CHUNK, s2, MASK_VALUE)
    mx = jnp.maximum(jnp.max(s1, -1, keepdims=True), jnp.max(s2, -1, keepdims=True))
    p1 = jnp.exp2(s1 - mx)
    p2 = jnp.exp2(s2 - mx)
    l = jnp.sum(p1, -1, keepdims=True) + jnp.sum(p2, -1, keepdims=True)
    o = (jnp.dot(p1.astype(BF16), kc, preferred_element_type=F32)
         + jnp.dot(p2.astype(BF16), kn[:, :MLA_RANK], preferred_element_type=F32)) * (1.0 / l)
    _uv_project(o, wuv_ref, o_ref, nq)


def _attn_sample(q, k, cache_ckv, cache_krt, lw, batch, nq):
    past_len = cache_ckv.shape[2]
    layer = lw.layer
    return pl.pallas_call(
        functools.partial(_attn_sample_kernel, past_len), grid=(batch,),
        in_specs=[pl.BlockSpec((MLA_HEADS, nq, QK_WIDTH), lambda b: (0, b, 0)),
                  pl.BlockSpec((nq, QK_WIDTH), lambda b: (b, 0)),
                  pl.BlockSpec((None, None, past_len, MLA_RANK), lambda b: (layer, b, 0, 0)),
                  pl.BlockSpec((None, None, MLA_D_ROPE, past_len), lambda b: (layer, b, 0, 0)),
                  lw.spec("w_uvp")],
        out_specs=pl.BlockSpec((nq, MLA_HEADS * MLA_D_V), lambda b: (b, 0)),
        out_shape=jax.ShapeDtypeStruct((batch * nq, MLA_HEADS * MLA_D_V), BF16),
        compiler_params=_params("parallel"), name="attn_sample",
    )(q, k, cache_ckv, cache_krt, lw["w_uvp"])


def _hgrn_levels(lc):
    spans = []
    sp = lc // 2
    while sp >= 1:
        spans.append(sp)
        sp //= 2
    return spans


def _hgrn_consts(lc):
    spans = _hgrn_levels(lc)
    tri = np.tril(np.ones((lc, lc), np.float32))
    t = np.arange(lc)
    mats, masks = [tri], []
    for sp in spans:
        blk = t // (2 * sp)
        ref_row = blk * 2 * sp + sp - 1
        mats.append(np.abs(tri - tri[ref_row]))
        right = (t % (2 * sp)) >= sp
        masks.append(((blk[:, None] == blk[None, :]) & right[:, None] & ~right[None, :]).astype(np.float32))
    return np.tile(np.concatenate(mats, 0), (1, LOGF_SPLIT)), np.stack(masks, 0)


def _hgrn_kernel(has_init, q_ref, k_ref, lf_ref, v_ref, gt_ref, gn_ref, cmat_ref, mask_ref, *rest):
    s0_ref = rest[0] if has_init else None
    o_ref, sout_ref, st_ref = rest[-3:]
    c = pl.program_id(1)
    nb, lc, _ = q_ref.shape
    nlev = mask_ref.shape[0]
    nt = (((1,), (1,)), ((), ()))
    tn = (((0,), (0,)), ((), ()))

    @pl.when(c == 0)
    def _():
        for bi in range(nb):
            for h in range(HG_HEADS):
                if has_init:
                    st_ref[bi, h] = s0_ref[bi, h].T
                else:
                    st_ref[bi, h] = jnp.zeros((HG_DV, HG_DK), F32)

    eye = (lax.broadcasted_iota(jnp.int32, (lc, lc), 0) == lax.broadcasted_iota(jnp.int32, (lc, lc), 1))
    lvl_mask = [mask_ref[lv] > 0.5 for lv in range(nlev)]
    for bi in range(nb):
        lf = lf_ref[bi] * LOG2E
        parts, rem = [], lf
        for _ in range(LOGF_SPLIT):
            parts.append(rem.astype(BF16))
            rem = rem - parts[-1].astype(F32)
        br = jnp.dot(cmat_ref[...], jnp.concatenate(parts, 0), preferred_element_type=F32)
        for h in range(HG_HEADS):
            sl = slice(h * HG_DK, (h + 1) * HG_DK)
            bh = br[:lc, sl]
            qh = q_ref[bi, :, sl]
            kh = k_ref[bi, :, sl]
            vh = v_ref[bi, :, sl]
            a = jnp.where(eye, jnp.sum(qh * kh, -1, keepdims=True), 0.0)
            for lv in range(nlev):
                e = jnp.exp2(br[(1 + lv) * lc:(2 + lv) * lc, sl])
                a = jnp.where(lvl_mask[lv],
                              lax.dot_general((qh * e).astype(BF16), (kh * e).astype(BF16), nt,
                                              preferred_element_type=F32), a)
            st = st_ref[bi, h]
            o = (jnp.dot(a.astype(BF16), vh, preferred_element_type=F32)
                 + lax.dot_general((qh * jnp.exp2(bh)).astype(BF16), st.astype(BF16), nt,
                                   preferred_element_type=F32))
            b_last = bh[lc - 1:lc, :]
            kdec = (kh * jnp.exp2(b_last - bh)).astype(BF16)
            st_ref[bi, h] = st * jnp.exp2(b_last) + lax.dot_general(vh, kdec, tn, preferred_element_type=F32)
            on = _rms_norm(o, gn_ref[:, sl])
            o_ref[bi, :, sl] = (on * gt_ref[bi, :, sl].astype(F32)).astype(BF16)

    @pl.when(c == pl.num_programs(1) - 1)
    def _():
        for bi in range(nb):
            for h in range(HG_HEADS):
                sout_ref[bi, h] = st_ref[bi, h].T


def _hgrn(hq, hk, hlf, hv, hgt, lw, state, batch, seq, lc, nb):
    nch = seq // lc
    assert batch % nb == 0
    cmat, masks = _hgrn_consts(lc)
    has_init = state is not None
    layer = lw.layer
    row = pl.BlockSpec((nb, lc, HG_W), lambda b, c: (b, c, 0))
    st_spec = pl.BlockSpec((nb, HG_HEADS, HG_DK, HG_DV), lambda b, c: (b, 0, 0, 0))
    in_specs = [row, row, row, row, row, lw.spec("hg_norm_g"), _const_spec(cmat.shape), _const_spec(masks.shape)]
    args = [t.reshape(batch, seq, HG_W) for t in (hq, hk, hlf, hv, hgt)]
    args += [lw["hg_norm_g"], jnp.asarray(cmat, BF16), jnp.asarray(masks)]
    if has_init:
        in_specs.append(pl.BlockSpec((None, nb, HG_HEADS, HG_DK, HG_DV), lambda b, c: (layer, b, 0, 0, 0)))
        args.append(state)
    o, s_fin = pl.pallas_call(
        functools.partial(_hgrn_kernel, has_init), grid=(batch // nb, nch),
        in_specs=in_specs, out_specs=[row, st_spec],
        out_shape=[jax.ShapeDtypeStruct((batch, seq, HG_W), BF16),
                   jax.ShapeDtypeStruct((batch, HG_HEADS, HG_DK, HG_DV), F32)],
        scratch_shapes=[pltpu.VMEM((nb, HG_HEADS, HG_DV, HG_DK), F32)],
        compiler_params=_params("parallel", "arbitrary"), name="hgrn",
    )(*args)
    return o.reshape(batch * seq, HG_W), s_fin


def _merge_kernel(alpha, cml, x_ref, ain_ref, hg_ref, u_ref, vcm_ref, sga_ref, sgb_ref, sgc_ref,
                  wpa_ref, wpb_ref, wpc_ref, wo_ref, ws_ref, bs_ref, g_ref, b_ref,
                  o_ref, cin_ref):
    tm = x_ref.shape[0]
    tril = (lax.broadcasted_iota(jnp.int32, (cml, cml), 0) >= lax.broadcasted_iota(jnp.int32, (cml, cml), 1))
    for g in range(CM_GROUPS):
        sl = slice(g * CM_GROUP_DIM, (g + 1) * CM_GROUP_DIM)
        wg = jnp.where(tril, ws_ref[g, :cml, :cml], 0.0).astype(BF16)
        for r in range(tm // cml):
            rs = slice(r * cml, (r + 1) * cml)
            s = jnp.dot(wg, vcm_ref[rs, sl].astype(BF16), preferred_element_type=F32) + bs_ref[:cml, sl]
            cin_ref[rs, sl] = (u_ref[rs, sl].astype(F32) * s).astype(BF16)
    y_a = jnp.dot(ain_ref[...], wpa_ref[...], preferred_element_type=F32)
    y_b = jnp.dot(hg_ref[...], wpb_ref[...], preferred_element_type=F32)
    y_c = jnp.dot(cin_ref[...], wpc_ref[...], preferred_element_type=F32)
    mrg = (sga_ref[...].astype(F32) * y_a + sgb_ref[...].astype(F32) * y_b
           + sgc_ref[...].astype(F32) * y_c).astype(BF16)
    y = alpha * x_ref[...] + jnp.dot(mrg, wo_ref[...], preferred_element_type=F32)
    o_ref[...] = _layer_norm(y, g_ref[...], b_ref[...])


def _merge(x, a_in, hg_o, u, vcm, sga, sgb, sgc, lw, alpha, cml, tm):
    m, d = x.shape
    row = lambda w: pl.BlockSpec((tm, w), lambda i: (i, 0))
    in_specs = [row(d), row(MLA_HEADS * MLA_D_V), row(HG_W), row(CM_WIDTH), row(CM_WIDTH), row(d), row(d), row(d),
                lw.spec("w_pa"), lw.spec("w_pb"), lw.spec("w_pc"), lw.spec("w_o"),
                lw.spec("cm_ws"), lw.spec("cm_bs_full"), lw.spec("ln1_g"), lw.spec("ln1_b")]
    return pl.pallas_call(
        functools.partial(_merge_kernel, alpha, cml), grid=(m // tm,), in_specs=in_specs,
        out_specs=row(d), out_shape=jax.ShapeDtypeStruct((m, d), F32),
        scratch_shapes=[pltpu.VMEM((tm, CM_WIDTH), BF16)],
        compiler_params=_params("parallel"), name="merge",
    )(x, a_in, hg_o, u, vcm, sga, sgb, sgc, lw["w_pa"], lw["w_pb"], lw["w_pc"], lw["w_o"],
      lw["cm_ws"], lw["cm_bs_full"], lw["ln1_g"], lw["ln1_b"])


def _ffn_kernel(alpha, nsplit, x_ref, wup_ref, wdn_ref, g_ref, b_ref, o_ref):
    x = x_ref[...]
    xb = x.astype(BF16)
    dff = wdn_ref.shape[0]
    cw = dff // nsplit
    y = alpha * x
    for j in range(nsplit):
        gate = jnp.dot(xb, wup_ref[:, j * cw:(j + 1) * cw], preferred_element_type=F32)
        up = jnp.dot(xb, wup_ref[:, dff + j * cw:dff + (j + 1) * cw], preferred_element_type=F32)
        act = (_silu(gate) * up).astype(BF16)
        y = y + jnp.dot(act, wdn_ref[j * cw:(j + 1) * cw, :], preferred_element_type=F32)
    o_ref[...] = _layer_norm(y, g_ref[...], b_ref[...])


def _ffn(x, lw, alpha, tm):
    m, d = x.shape
    row = pl.BlockSpec((tm, d), lambda i: (i, 0))
    return pl.pallas_call(
        functools.partial(_ffn_kernel, alpha, 1), grid=(m // tm,),
        in_specs=[row, lw.spec("w_up"), lw.spec("w_down"), lw.spec("ln2_g"), lw.spec("ln2_b")],
        out_specs=row, out_shape=jax.ShapeDtypeStruct((m, d), F32),
        compiler_params=_params("parallel"), name="ffn",
    )(x, lw["w_up"], lw["w_down"], lw["ln2_g"], lw["ln2_b"])


def _rope_table(pos):
    half = MLA_D_ROPE // 2
    inv = 1.0 / (ROPE_THETA ** (jnp.arange(half, dtype=F32) / half))
    ang = pos.astype(F32)[:, None] * inv[None]
    cos, sin = jnp.cos(ang), jnp.sin(ang)
    z = jnp.zeros((pos.shape[0], LANES - MLA_D_ROPE), F32)
    return jnp.concatenate([cos, cos, z, -sin, sin, z], -1)


def _prep_weights(w_in, q_norm_g, w_uq, w_uk, kv_norm_g, w_uv, hg_lb, hg_norm_g, cm_ln_g, cm_ln_b,
                  cm_ws, cm_bs, w_pa, w_pb, w_pc, w_o, ln1_g, ln1_b, w_up, w_down, ln2_g, ln2_b):
    depth, d, _ = w_in.shape
    half = MLA_D_ROPE // 2
    o_kr = 2 * MLA_RANK
    assert w_in.shape[2] - o_kr - MLA_D_ROPE == _B_END
    w_a = w_in[:, :, :o_kr].astype(BF16)
    w_b = w_in[:, :, o_kr + MLA_D_ROPE:].astype(BF16)
    kr = w_in[:, :, o_kr:o_kr + MLA_D_ROPE]
    zk = jnp.zeros((depth, d, LANES - MLA_D_ROPE), F32)
    w_kr = jnp.concatenate([kr, zk, kr[..., half:], kr[..., :half], zk], -1).astype(BF16)

    dq = MLA_D_NOPE + MLA_D_ROPE
    uq = w_uq.reshape(depth, MLA_RANK, MLA_HEADS, dq)
    nope = uq[..., :MLA_D_NOPE]
    r1, r2 = uq[..., MLA_D_NOPE:MLA_D_NOPE + half], uq[..., MLA_D_NOPE + half:]
    zn = jnp.zeros((depth, MLA_RANK, MLA_HEADS, LANES - MLA_D_NOPE), F32)
    zr = jnp.zeros((depth, MLA_RANK, MLA_HEADS, LANES - MLA_D_ROPE), F32)
    flat = lambda t: t.reshape(depth, MLA_RANK, MLA_HEADS * LANES)
    w_uq_r = jnp.concatenate([flat(jnp.concatenate([nope, zn], -1)),
                              flat(jnp.concatenate([r1, r2, zr], -1)),
                              flat(jnp.concatenate([r2, r1, zr], -1))], -1).astype(BF16)
    ukt = jnp.transpose(w_uk, (0, 2, 3, 1))
    w_uk_r = jnp.concatenate([ukt, jnp.zeros((depth, MLA_HEADS, LANES - MLA_D_NOPE, MLA_RANK), F32)], 2).astype(BF16)
    uvt = jnp.transpose(w_uv, (0, 2, 1, 3)).astype(BF16)
    lane_half = (jnp.arange(2 * MLA_D_V) // MLA_D_V)[None, None, None, :]
    head_half = (jnp.arange(MLA_HEADS) % 2)[None, :, None, None]
    w_uvp = jnp.where(lane_half == head_half, jnp.concatenate([uvt, uvt], -1), jnp.zeros((), BF16))

    sm = jax.nn.softmax(hg_lb.astype(F32), axis=0)
    lb = jnp.concatenate([jnp.zeros_like(sm[:1]), jnp.cumsum(sm[1:], axis=0)], axis=0)
    lbp = jnp.stack([jnp.log(lb + LB_TINY), jnp.log1p(-lb), 1.0 - lb], 1)

    bs_full = jnp.repeat(jnp.transpose(cm_bs, (0, 2, 1)), CM_GROUP_DIM, axis=-1)

    r3 = lambda t: t.reshape(depth, 1, -1)
    return dict(w_a=w_a, w_kr=w_kr, w_b=w_b, w_uq=w_uq_r, w_uk=w_uk_r, w_uvp=w_uvp, lbp=lbp,
                q_norm_g=r3(q_norm_g), kv_norm_g=r3(kv_norm_g), hg_norm_g=r3(hg_norm_g),
                cm_ln_g=r3(cm_ln_g), cm_ln_b=r3(cm_ln_b), cm_ws=cm_ws, cm_bs_full=bs_full,
                w_pa=w_pa.astype(BF16), w_pb=w_pb.astype(BF16), w_pc=w_pc.astype(BF16), w_o=w_o.astype(BF16),
                ln1_g=r3(ln1_g), ln1_b=r3(ln1_b), w_up=w_up.astype(BF16), w_down=w_down.astype(BF16),
                ln2_g=r3(ln2_g), ln2_b=r3(ln2_b))


def _hgrn_rows(batch):
    for nb in (8, 4, 2, 1):
        if batch % nb == 0:
            return nb


def _row_tile(m, want):
    tm = min(m, want)
    assert m % tm == 0, (m, tm)
    return tm


def _layer(x, lw, rope_tab, alpha, batch, seq, past, prev, emb, tm):
    prompt = past is None
    outs = _proj(x, lw, rope_tab, tm, BF16 if prompt else F32, prev, emb)
    if emb is not None:
        x = outs[-1]
    q, k, ckv, kr, hq, hk, hlf, hv, hgt, u, vcm, sga, sgb, sgc = outs[:14]
    if prompt:
        a_in = _attn_prompt(q, k, lw, batch, seq, Q_BLOCK)
        hg_o, s_fin = _hgrn(hq, hk, hlf, hv, hgt, lw, None, batch, seq, CHUNK, _hgrn_rows(batch))
    else:
        a_in = _attn_sample(q, k, past[0], past[1], lw, batch, seq)
        hg_o, s_fin = _hgrn(hq, hk, hlf, hv, hgt, lw, past[2], batch, seq, seq, _hgrn_rows(batch))
    cml = min(seq, CM_CHUNK)
    tm_wide = _row_tile(x.shape[0], WIDE_ROW_TILE)
    x = _merge(x, a_in, hg_o, u, vcm, sga, sgb, sgc, lw, alpha, cml, tm_wide)
    x = _ffn(x, lw, alpha, tm_wide)
    return x, (ckv, kr), (s_fin, vcm)


def kernel(x_prompt, x_sample, cache_mla_ckv, cache_mla_krope, state_hgrn, emb_ln_g, emb_ln_b, w_in, q_norm_g, w_uq, w_uk, kv_norm_g, w_uv, hg_lb, hg_norm_g, cm_ln_g, cm_ln_b, cm_ws, cm_bs, w_pa, w_pb, w_pc, w_o, ln1_g, ln1_b, w_up, w_down, ln2_g, ln2_b):
    bp, sp, d = x_prompt.shape
    bs, ss, _ = x_sample.shape
    depth = w_in.shape[0]
    past_len = cache_mla_ckv.shape[2]
    assert sp % CM_CHUNK == 0 and sp % KV_BLOCK == 0 and ss <= CHUNK and ss % 16 == 0
    alpha = float((2 * depth) ** 0.25)

    wts = _prep_weights(w_in, q_norm_g, w_uq, w_uk, kv_norm_g, w_uv, hg_lb, hg_norm_g, cm_ln_g, cm_ln_b,
                        cm_ws, cm_bs, w_pa, w_pb, w_pc, w_o, ln1_g, ln1_b, w_up, w_down, ln2_g, ln2_b)
    tm_p = _row_tile(bp * sp, ROW_TILE)
    tm_s = _row_tile(bs * ss, ROW_TILE)
    assert (sp % tm_p == 0 or tm_p % sp == 0) and (ss % tm_s == 0 or tm_s % ss == 0)
    rope_p = jnp.tile(_rope_table(jnp.arange(sp, dtype=jnp.int32)), (max(1, tm_p // sp), 1))
    rope_s = jnp.tile(_rope_table(past_len + jnp.arange(ss, dtype=jnp.int32)), (max(1, tm_s // ss), 1))
    emb = (emb_ln_g.reshape(1, d), emb_ln_b.reshape(1, d))
    xp = x_prompt.reshape(bp * sp, d)
    xs = x_sample.reshape(bs * ss, d)
    past = (cache_mla_ckv, jnp.swapaxes(cache_mla_krope, -1, -2), state_hgrn)

    stacked = lambda m: (jnp.zeros((depth, m, MLA_RANK), F32), jnp.zeros((depth, m, MLA_D_ROPE), F32))
    kv_p, kv_s = stacked(bp * sp), stacked(bs * ss)
    st_p, st_s, v_s = [], [], []
    for l in range(depth):
        lw = _LayerWeights(wts, l)
        xp, kv_p, (s_p, _) = _layer(xp, lw, rope_p, alpha, bp, sp, None, kv_p, emb if l == 0 else None, tm_p)
        xs, kv_s, (s_s, v) = _layer(xs, lw, rope_s, alpha, bs, ss, past, kv_s, emb if l == 0 else None, tm_s)
        st_p.append(s_p)
        st_s.append(s_s)
        v_s.append(v.reshape(bs, ss, -1))
    return (xp.reshape(bp, sp, d), xs.reshape(bs, ss, d),
            kv_p[0].reshape(depth, bp, sp, -1), kv_p[1].reshape(depth, bp, sp, -1), jnp.stack(st_p),
            kv_s[0].reshape(depth, bs, ss, -1), kv_s[1].reshape(depth, bs, ss, -1), jnp.stack(st_s),
            jnp.stack(v_s))
```

```python
import functools

import numpy as np
import jax
import jax.numpy as jnp
from jax import lax
from jax.experimental import pallas as pl
from jax.experimental.pallas import tpu as pltpu

F32 = jnp.float32
BF16 = jnp.bfloat16

CHUNK = 64
MLA_HEADS = 8
MLA_D_NOPE = 64
MLA_D_ROPE = 32
MLA_D_V = 64
MLA_RANK = 256
MLA_SCALE = (MLA_D_NOPE + MLA_D_ROPE) ** -0.5
LOG2E = float(np.log2(np.e))
Q_SCALE = MLA_SCALE * LOG2E
ROPE_THETA = 10000.0
MASK_VALUE = -1e30
HG_HEADS = 4
HG_DK = 128
HG_DV = 128
HG_W = HG_HEADS * HG_DK
LB_TINY = 1e-30
LOGF_SPLIT = 3
CM_CHUNK = 128
CM_GROUPS = 4
CM_WIDTH = 512
CM_GROUP_DIM = CM_WIDTH // CM_GROUPS
EPS = 1e-5

LANES = 128
VMEM_LIMIT_BYTES = 56 * 1024 * 1024
QK_WIDTH = MLA_RANK + LANES
KV_BLOCK = 256
Q_BLOCK = 256
ROW_TILE = 256
WIDE_ROW_TILE = 512

_A_CQ, _A_CKV, _A_END = 0, 256, 512
_B_HQ, _B_HF, _B_HI, _B_HG, _B_CU, _B_CV, _B_GA, _B_GB, _B_GC, _B_END = (
    0, 512, 1024, 1536, 2048, 2560, 3072, 4096, 5120, 6144)


def _const_spec(shape):
    nd = len(shape)
    return pl.BlockSpec(shape, lambda *_: (0,) * nd, pipeline_mode=pl.Buffered(1))


class _LayerWeights:
    def __init__(self, stacked, layer):
        self.stacked = stacked
        self.layer = layer

    def __getitem__(self, name):
        return self.stacked[name]

    def spec(self, name):
        shape = self.stacked[name].shape[1:]
        layer, nd = self.layer, len(shape)
        return pl.BlockSpec((None,) + shape, lambda *_: (layer,) + (0,) * nd, pipeline_mode=pl.Buffered(1))


def _params(*sem):
    return pltpu.CompilerParams(dimension_semantics=sem, vmem_limit_bytes=VMEM_LIMIT_BYTES)


def _layer_norm(x, g, b):
    mu = jnp.mean(x, -1, keepdims=True)
    xc = x - mu
    var = jnp.mean(xc * xc, -1, keepdims=True)
    return xc * lax.rsqrt(var + EPS) * g + b


def _rms_norm(x, g):
    return x * lax.rsqrt(jnp.mean(x * x, -1, keepdims=True) + EPS) * g


def _gelu(x):
    return 0.5 * x * (1.0 + lax.erf(x * np.float32(1.0 / np.sqrt(2.0))))


def _sigmoid(x):
    return 1.0 / (1.0 + jnp.exp(-x))


def _silu(x):
    return x * _sigmoid(x)


def _log_sigmoid(x):
    return jnp.minimum(x, 0.0) - jnp.log(1.0 + jnp.exp(-jnp.abs(x)))


def _proj_kernel(n_alias, pre_ln, x_ref, wa_ref, wkr_ref, wb_ref, wuq_ref, wuk_ref, qg_ref, kvg_ref, rope_ref,
                 lbp_ref, cmg_ref, cmb_ref, *rest):
    (q_ref, kv_ref, ckv_ref, kr_ref, hq_ref, hk_ref, hlf_ref, hv_ref, hgt_ref,
     u_ref, vcm_ref, sga_ref, sgb_ref, sgc_ref) = rest[n_alias + 2 * pre_ln:][:14]
    x = x_ref[...]
    if pre_ln:
        x = _layer_norm(x, rest[0][...], rest[1][...])
        rest[-1][...] = x
    xb = x.astype(BF16)

    def proj_a(lo, hi):
        return jnp.dot(xb, wa_ref[:, lo:hi], preferred_element_type=F32)

    def proj(lo, hi):
        return jnp.dot(xb, wb_ref[:, lo:hi], preferred_element_type=F32)

    cos_t = rope_ref[:, :LANES]
    sin_up = rope_ref[:, LANES:2 * LANES]
    sin_dn = rope_ref[:, 2 * LANES:]
    half = MLA_D_ROPE // 2

    def rotary(t):
        return (t * cos_t + pltpu.roll(t, half, axis=1) * sin_up
                + pltpu.roll(t, LANES - half, axis=1) * sin_dn)

    cqn = _rms_norm(proj_a(_A_CQ, _A_CKV), qg_ref[...]).astype(BF16)
    q3 = jnp.dot(cqn, wuq_ref[...], preferred_element_type=F32)
    hw = MLA_HEADS * LANES
    for h in range(MLA_HEADS):
        nope = q3[:, h * LANES:(h + 1) * LANES].astype(BF16)
        lat = jnp.dot(nope, wuk_ref[h], preferred_element_type=F32) * Q_SCALE
        rot = rotary(q3[:, hw + h * LANES:hw + (h + 1) * LANES]) * Q_SCALE
        q_ref[h, :, :MLA_RANK] = lat.astype(BF16)
        q_ref[h, :, MLA_RANK:] = rot.astype(BF16)

    ckv = _rms_norm(proj_a(_A_CKV, _A_END), kvg_ref[...])
    ckv_ref[...] = ckv
    kv_ref[:, :MLA_RANK] = ckv.astype(BF16)
    zk = jnp.dot(xb, wkr_ref[...], preferred_element_type=F32)
    krot = rotary(zk)
    kr_ref[...] = krot[:, :MLA_D_ROPE]
    kv_ref[:, MLA_RANK:] = krot.astype(BF16)

    hq_ref[...] = proj(_B_HQ, _B_HF)
    zf = proj(_B_HF, _B_HI)
    log_lb = lbp_ref[0:1, :]
    log_1m = lbp_ref[1:2, :]
    one_m = lbp_ref[2:3, :]
    c = log_1m + _log_sigmoid(zf)
    hlf_ref[...] = jnp.maximum(log_lb, c) + jnp.log(1.0 + jnp.exp(-jnp.abs(log_lb - c)))
    hk_ref[...] = one_m * _sigmoid(-zf)
    hv_ref[...] = proj(_B_HI, _B_HG).astype(BF16)
    hgt_ref[...] = _silu(proj(_B_HG, _B_CU)).astype(BF16)

    u_ref[...] = _gelu(proj(_B_CU, _B_CV)).astype(BF16)
    vcm_ref[...] = _layer_norm(_gelu(proj(_B_CV, _B_GA)), cmg_ref[...], cmb_ref[...]).astype(vcm_ref.dtype)

    sga_ref[...] = _sigmoid(proj(_B_GA, _B_GB)).astype(BF16)
    sgb_ref[...] = _sigmoid(proj(_B_GB, _B_GC)).astype(BF16)
    sgc_ref[...] = _sigmoid(proj(_B_GC, _B_END)).astype(BF16)


def _proj(x, lw, rope_tab, tm, vcm_dtype, prev, emb):
    m, d = x.shape
    depth = lw["w_b"].shape[0]
    layer = lw.layer
    row = lambda w: pl.BlockSpec((tm, w), lambda i: (i, 0))
    lrow = lambda w: pl.BlockSpec((None, tm, w), lambda i: (layer, i, 0))
    out_shape = [
        jax.ShapeDtypeStruct((MLA_HEADS, m, QK_WIDTH), BF16),
        jax.ShapeDtypeStruct((m, QK_WIDTH), BF16),
        jax.ShapeDtypeStruct((depth, m, MLA_RANK), F32),
        jax.ShapeDtypeStruct((depth, m, MLA_D_ROPE), F32),
        jax.ShapeDtypeStruct((m, HG_W), F32),
        jax.ShapeDtypeStruct((m, HG_W), F32),
        jax.ShapeDtypeStruct((m, HG_W), F32),
        jax.ShapeDtypeStruct((m, HG_W), BF16),
        jax.ShapeDtypeStruct((m, HG_W), BF16),
        jax.ShapeDtypeStruct((m, CM_WIDTH), BF16),
        jax.ShapeDtypeStruct((m, CM_WIDTH), vcm_dtype),
        jax.ShapeDtypeStruct((m, d), BF16),
        jax.ShapeDtypeStruct((m, d), BF16),
        jax.ShapeDtypeStruct((m, d), BF16),
    ]
    out_specs = [
        pl.BlockSpec((MLA_HEADS, tm, QK_WIDTH), lambda i: (0, i, 0)),
        row(QK_WIDTH), lrow(MLA_RANK), lrow(MLA_D_ROPE),
        row(HG_W), row(HG_W), row(HG_W), row(HG_W), row(HG_W),
        row(CM_WIDTH), row(CM_WIDTH), row(d), row(d), row(d),
    ]
    names = ("w_a", "w_kr", "w_b", "w_uq", "w_uk", "q_norm_g", "kv_norm_g")
    tail = ("lbp", "cm_ln_g", "cm_ln_b")
    in_specs = ([row(d)] + [lw.spec(n) for n in names]
                + [pl.BlockSpec((tm, 3 * LANES), lambda i: (i % (rope_tab.shape[0] // tm), 0))]
                + [lw.spec(n) for n in tail])
    args = [x] + [lw[n] for n in names] + [rope_tab] + [lw[n] for n in tail]
    if emb is not None:
        in_specs += [_const_spec((1, d))] * 2
        args += list(emb)
        out_shape.append(jax.ShapeDtypeStruct((m, d), F32))
        out_specs.append(row(d))
    aliases = {len(args): 2, len(args) + 1: 3}
    in_specs += [pl.BlockSpec(memory_space=pl.ANY)] * 2
    args += list(prev)
    return pl.pallas_call(
        functools.partial(_proj_kernel, len(aliases), emb is not None), grid=(m // tm,), in_specs=in_specs,
        out_specs=out_specs, out_shape=out_shape, input_output_aliases=aliases,
        compiler_params=_params("parallel"), name="proj",
    )(*args)


def _uv_project(o, wuv_ref, o_ref, tq):
    ob = o.astype(BF16)
    for i in range(MLA_HEADS // 2):
        pair = (jnp.dot(ob[2 * i * tq:(2 * i + 1) * tq], wuv_ref[2 * i], preferred_element_type=F32)
                + jnp.dot(ob[(2 * i + 1) * tq:(2 * i + 2) * tq], wuv_ref[2 * i + 1], preferred_element_type=F32))
        o_ref[:, i * LANES:(i + 1) * LANES] = pair.astype(BF16)


def _attn_prompt_kernel(q_ref, k_ref, wuv_ref, o_ref, m_ref, l_ref, acc_ref):
    qb = pl.program_id(1)
    tq = q_ref.shape[1]
    rows = MLA_HEADS * tq
    q = q_ref[...].reshape(rows, QK_WIDTH)
    kv_len = (qb + 1) * tq
    nblk = (kv_len + KV_BLOCK - 1) // KV_BLOCK
    rep = KV_BLOCK // LANES

    def scores(j):
        start = pl.multiple_of(j * KV_BLOCK, KV_BLOCK)
        return lax.dot_general(q, k_ref[pl.ds(start, KV_BLOCK), :], (((1,), (1,)), ((), ())),
                               preferred_element_type=F32)

    def values(j):
        start = pl.multiple_of(j * KV_BLOCK, KV_BLOCK)
        return k_ref[pl.ds(start, KV_BLOCK), :MLA_RANK]

    def lane_sums(p):
        return sum(p[:, i * LANES:(i + 1) * LANES] for i in range(rep))

    def update(j, s):
        m_prev = m_ref[...]
        m_new = jnp.maximum(m_prev, jnp.max(s, -1, keepdims=True))
        alpha = jnp.exp2(m_prev - m_new)
        p = jnp.exp2(s - jnp.tile(m_new, (1, rep)))
        l_ref[...] = alpha * l_ref[...] + lane_sums(p)
        acc_ref[...] = jnp.tile(alpha, (1, MLA_RANK // LANES)) * acc_ref[...] + jnp.dot(
            p.astype(BF16), values(j), preferred_element_type=F32)
        m_ref[...] = m_new

    s_diag = scores(nblk - 1)
    s_first = scores(0)
    lane = lax.broadcasted_iota(jnp.int32, (CHUNK, KV_BLOCK), 1)
    visible = [lane < qb * tq + (c + 1) * CHUNK - (nblk - 1) * KV_BLOCK for c in range(tq // CHUNK)]
    s_diag = jnp.concatenate(
        [jnp.where(visible[c], s_diag[h * tq + c * CHUNK:h * tq + (c + 1) * CHUNK], MASK_VALUE)
         for h in range(MLA_HEADS) for c in range(tq // CHUNK)], 0)
    m_diag = jnp.broadcast_to(jnp.max(s_diag, -1, keepdims=True), m_ref.shape)
    p_diag = jnp.exp2(s_diag - jnp.tile(m_diag, (1, rep)))
    m_ref[...] = m_diag
    l_ref[...] = lane_sums(p_diag)
    acc_ref[...] = jnp.dot(p_diag.astype(BF16), values(nblk - 1), preferred_element_type=F32)

    def body(j, s):
        s_next = scores(j + 1)
        update(j, s)
        return s_next

    s_last = lax.fori_loop(0, nblk - 2, body, s_first)

    @pl.when(nblk >= 2)
    def _():
        update(nblk - 2, s_last)

    l_tot = jnp.broadcast_to(jnp.sum(l_ref[...], -1, keepdims=True), l_ref.shape)
    o = acc_ref[...] * jnp.tile(1.0 / l_tot, (1, MLA_RANK // LANES))
    _uv_project(o, wuv_ref, o_ref, tq)


def _attn_prompt(q, k, lw, batch, seq, tq):
    m = batch * seq
    nqc = seq // tq
    rows = MLA_HEADS * tq
    assert tq & (tq - 1) == 0 and tq % CHUNK == 0 and seq % tq == 0
    return pl.pallas_call(
        _attn_prompt_kernel, grid=(batch, nqc),
        in_specs=[pl.BlockSpec((MLA_HEADS, tq, QK_WIDTH), lambda b, c: (0, b * nqc + c, 0)),
                  pl.BlockSpec((seq, QK_WIDTH), lambda b, c: (b, 0)), lw.spec("w_uvp")],
        out_specs=pl.BlockSpec((tq, MLA_HEADS * MLA_D_V), lambda b, c: (b * nqc + c, 0)),
        out_shape=jax.ShapeDtypeStruct((m, MLA_HEADS * MLA_D_V), BF16),
        scratch_shapes=[pltpu.VMEM((rows, LANES), F32), pltpu.VMEM((rows, LANES), F32),
                        pltpu.VMEM((rows, MLA_RANK), F32)],
        compiler_params=_params("parallel", "arbitrary"), name="attn_prompt",
    )(q, k, lw["w_uvp"])


def _attn_sample_kernel(past_len, q_ref, k_ref, cckv_ref, ckrt_ref, wuv_ref, o_ref):
    nq = q_ref.shape[1]
    rows = MLA_HEADS * nq
    q = q_ref[...].reshape(rows, QK_WIDTH)
    q_lat = q[:, :MLA_RANK]
    q_rot = q[:, MLA_RANK:MLA_RANK + MLA_D_ROPE]
    nt = (((1,), (1,)), ((), ()))
    kc = cckv_ref[...].astype(BF16)
    s1 = (lax.dot_general(q_lat, kc, nt, preferred_element_type=F32)
          + jnp.dot(q_rot, ckrt_ref[...].astype(BF16), preferred_element_type=F32))
    kn = k_ref[...]
    s2 = lax.dot_general(q, kn, nt, preferred_element_type=F32)
    q_pos = past_len + lax.broadcasted_iota(jnp.int32, s2.shape, 0) % nq
    k_pos = past_len + lax.broadcasted_iota(jnp.int32, s2.shape, 1)
    s2 = jnp.where(k_pos // CHUNK <= q_pos // CHUNK, s2, MASK_VALUE)
    mx = jnp.maximum(jnp.max(s1, -1, keepdims=True), jnp.max(s2, -1, keepdims=True))
    p1 = jnp.exp2(s1 - mx)
    p2 = jnp.exp2(s2 - mx)
    l = jnp.sum(p1, -1, keepdims=True) + jnp.sum(p2, -1, keepdims=True)
    o = (jnp.dot(p1.astype(BF16), kc, preferred_element_type=F32)
         + jnp.dot(p2.astype(BF16), kn[:, :MLA_RANK], preferred_element_type=F32)) * (1.0 / l)
    _uv_project(o, wuv_ref, o_ref, nq)


def _attn_sample(q, k, cache_ckv, cache_krt, lw, batch, nq):
    past_len = cache_ckv.shape[2]
    layer = lw.layer
    return pl.pallas_call(
        functools.partial(_attn_sample_kernel, past_len), grid=(batch,),
        in_specs=[pl.BlockSpec((MLA_HEADS, nq, QK_WIDTH), lambda b: (0, b, 0)),
                  pl.BlockSpec((nq, QK_WIDTH), lambda b: (b, 0)),
                  pl.BlockSpec((None, None, past_len, MLA_RANK), lambda b: (layer, b, 0, 0)),
                  pl.BlockSpec((None, None, MLA_D_ROPE, past_len), lambda b: (layer, b, 0, 0)),
                  lw.spec("w_uvp")],
        out_specs=pl.BlockSpec((nq, MLA_HEADS * MLA_D_V), lambda b: (b, 0)),
        out_shape=jax.ShapeDtypeStruct((batch * nq, MLA_HEADS * MLA_D_V), BF16),
        compiler_params=_params("parallel"), name="attn_sample",
    )(q, k, cache_ckv, cache_krt, lw["w_uvp"])


def _hgrn_levels(lc):
    spans = []
    sp = lc // 2
    while sp >= 1:
        spans.append(sp)
        sp //= 2
    return spans


def _hgrn_consts(lc):
    spans = _hgrn_levels(lc)
    tri = np.tril(np.ones((lc, lc), np.float32))
    t = np.arange(lc)
    mats, masks = [tri], []
    for sp in spans:
        blk = t // (2 * sp)
        ref_row = blk * 2 * sp + sp - 1
        mats.append(np.abs(tri - tri[ref_row]))
        right = (t % (2 * sp)) >= sp
        masks.append(((blk[:, None] == blk[None, :]) & right[:, None] & ~right[None, :]).astype(np.float32))
    return np.tile(np.concatenate(mats, 0), (1, LOGF_SPLIT)), np.stack(masks, 0)


def _hgrn_kernel(has_init, q_ref, k_ref, lf_ref, v_ref, gt_ref, gn_ref, cmat_ref, mask_ref, *rest):
    s0_ref = rest[0] if has_init else None
    o_ref, sout_ref, st_ref = rest[-3:]
    c = pl.program_id(1)
    nb, lc, _ = q_ref.shape
    nlev = mask_ref.shape[0]
    nt = (((1,), (1,)), ((), ()))
    tn = (((0,), (0,)), ((), ()))

    @pl.when(c == 0)
    def _():
        for bi in range(nb):
            for h in range(HG_HEADS):
                if has_init:
                    st_ref[bi, h] = s0_ref[bi, h].T
                else:
                    st_ref[bi, h] = jnp.zeros((HG_DV, HG_DK), F32)

    eye = (lax.broadcasted_iota(jnp.int32, (lc, lc), 0) == lax.broadcasted_iota(jnp.int32, (lc, lc), 1))
    lvl_mask = [mask_ref[lv] > 0.5 for lv in range(nlev)]
    for bi in range(nb):
        lf = lf_ref[bi] * LOG2E
        parts, rem = [], lf
        for _ in range(LOGF_SPLIT):
            parts.append(rem.astype(BF16))
            rem = rem - parts[-1].astype(F32)
        br = jnp.dot(cmat_ref[...], jnp.concatenate(parts, 0), preferred_element_type=F32)
        for h in range(HG_HEADS):
            sl = slice(h * HG_DK, (h + 1) * HG_DK)
            bh = br[:lc, sl]
            qh = q_ref[bi, :, sl]
            kh = k_ref[bi, :, sl]
            vh = v_ref[bi, :, sl]
            a = jnp.where(eye, jnp.sum(qh * kh, -1, keepdims=True), 0.0)
            for lv in range(nlev):
                e = jnp.exp2(br[(1 + lv) * lc:(2 + lv) * lc, sl])
                a = jnp.where(lvl_mask[lv],
                              lax.dot_general((qh * e).astype(BF16), (kh * e).astype(BF16), nt,
                                              preferred_element_type=F32), a)
            st = st_ref[bi, h]
            o = (jnp.dot(a.astype(BF16), vh, preferred_element_type=F32)
                 + lax.dot_general((qh * jnp.exp2(bh)).astype(BF16), st.astype(BF16), nt,
                                   preferred_element_type=F32))
            b_last = bh[lc - 1:lc, :]
            kdec = (kh * jnp.exp2(b_last - bh)).astype(BF16)
            st_ref[bi, h] = st * jnp.exp2(b_last) + lax.dot_general(vh, kdec, tn, preferred_element_type=F32)
            on = _rms_norm(o, gn_ref[:, sl])
            o_ref[bi, :, sl] = (on * gt_ref[bi, :, sl].astype(F32)).astype(BF16)

    @pl.when(c == pl.num_programs(1) - 1)
    def _():
        for bi in range(nb):
            for h in range(HG_HEADS):
                sout_ref[bi, h] = st_ref[bi, h].T


def _hgrn(hq, hk, hlf, hv, hgt, lw, state, batch, seq, lc, nb):
    nch = seq // lc
    assert batch % nb == 0
    cmat, masks = _hgrn_consts(lc)
    has_init = state is not None
    layer = lw.layer
    row = pl.BlockSpec((nb, lc, HG_W), lambda b, c: (b, c, 0))
    st_spec = pl.BlockSpec((nb, HG_HEADS, HG_DK, HG_DV), lambda b, c: (b, 0, 0, 0))
    in_specs = [row, row, row, row, row, lw.spec("hg_norm_g"), _const_spec(cmat.shape), _const_spec(masks.shape)]
    args = [t.reshape(batch, seq, HG_W) for t in (hq, hk, hlf, hv, hgt)]
    args += [lw["hg_norm_g"], jnp.asarray(cmat, BF16), jnp.asarray(masks)]
    if has_init:
        in_specs.append(pl.BlockSpec((None, nb, HG_HEADS, HG_DK, HG_DV), lambda b, c: (layer, b, 0, 0, 0)))
        args.append(state)
    o, s_fin = pl.pallas_call(
        functools.partial(_hgrn_kernel, has_init), grid=(batch // nb, nch),
        in_specs=in_specs, out_specs=[row, st_spec],
        out_shape=[jax.ShapeDtypeStruct((batch, seq, HG_W), BF16),
                   jax.ShapeDtypeStruct((batch, HG_HEADS, HG_DK, HG_DV), F32)],
        scratch_shapes=[pltpu.VMEM((nb, HG_HEADS, HG_DV, HG_DK), F32)],
        compiler_params=_params("parallel", "arbitrary"), name="hgrn",
    )(*args)
    return o.reshape(batch * seq, HG_W), s_fin


def _merge_kernel(alpha, cml, x_ref, ain_ref, hg_ref, u_ref, vcm_ref, sga_ref, sgb_ref, sgc_ref,
                  wpa_ref, wpb_ref, wpc_ref, wo_ref, ws_ref, bs_ref, g_ref, b_ref,
                  o_ref, cin_ref):
    tm = x_ref.shape[0]
    tril = (lax.broadcasted_iota(jnp.int32, (cml, cml), 0) >= lax.broadcasted_iota(jnp.int32, (cml, cml), 1))
    for g in range(CM_GROUPS):
        sl = slice(g * CM_GROUP_DIM, (g + 1) * CM_GROUP_DIM)
        wg = jnp.where(tril, ws_ref[g, :cml, :cml], 0.0).astype(BF16)
        for r in range(tm // cml):
            rs = slice(r * cml, (r + 1) * cml)
            s = jnp.dot(wg, vcm_ref[rs, sl].astype(BF16), preferred_element_type=F32) + bs_ref[:cml, sl]
            cin_ref[rs, sl] = (u_ref[rs, sl].astype(F32) * s).astype(BF16)
    y_a = jnp.dot(ain_ref[...], wpa_ref[...], preferred_element_type=F32)
    y_b = jnp.dot(hg_ref[...], wpb_ref[...], preferred_element_type=F32)
    y_c = jnp.dot(cin_ref[...], wpc_ref[...], preferred_element_type=F32)
    mrg = (sga_ref[...].astype(F32) * y_a + sgb_ref[...].astype(F32) * y_b
           + sgc_ref[...].astype(F32) * y_c).astype(BF16)
    y = alpha * x_ref[...] + jnp.dot(mrg, wo_ref[...], preferred_element_type=F32)
    o_ref[...] = _layer_norm(y, g_ref[...], b_ref[...])


def _merge(x, a_in, hg_o, u, vcm, sga, sgb, sgc, lw, alpha, cml, tm):
    m, d = x.shape
    row = lambda w: pl.BlockSpec((tm, w), lambda i: (i, 0))
    in_specs = [row(d), row(MLA_HEADS * MLA_D_V), row(HG_W), row(CM_WIDTH), row(CM_WIDTH), row(d), row(d), row(d),
                lw.spec("w_pa"), lw.spec("w_pb"), lw.spec("w_pc"), lw.spec("w_o"),
                lw.spec("cm_ws"), lw.spec("cm_bs_full"), lw.spec("ln1_g"), lw.spec("ln1_b")]
    return pl.pallas_call(
        functools.partial(_merge_kernel, alpha, cml), grid=(m // tm,), in_specs=in_specs,
        out_specs=row(d), out_shape=jax.ShapeDtypeStruct((m, d), F32),
        scratch_shapes=[pltpu.VMEM((tm, CM_WIDTH), BF16)],
        compiler_params=_params("parallel"), name="merge",
    )(x, a_in, hg_o, u, vcm, sga, sgb, sgc, lw["w_pa"], lw["w_pb"], lw["w_pc"], lw["w_o"],
      lw["cm_ws"], lw["cm_bs_full"], lw["ln1_g"], lw["ln1_b"])


def _ffn_kernel(alpha, nsplit, x_ref, wup_ref, wdn_ref, g_ref, b_ref, o_ref):
    x = x_ref[...]
    xb = x.astype(BF16)
    dff = wdn_ref.shape[0]
    cw = dff // nsplit
    y = alpha * x
    for j in range(nsplit):
        gate = jnp.dot(xb, wup_ref[:, j * cw:(j + 1) * cw], preferred_element_type=F32)
        up = jnp.dot(xb, wup_ref[:, dff + j * cw:dff + (j + 1) * cw], preferred_element_type=F32)
        act = (_silu(gate) * up).astype(BF16)
        y = y + jnp.dot(act, wdn_ref[j * cw:(j + 1) * cw, :], preferred_element_type=F32)
    o_ref[...] = _layer_norm(y, g_ref[...], b_ref[...])


def _ffn(x, lw, alpha, tm):
    m, d = x.shape
    row = pl.BlockSpec((tm, d), lambda i: (i, 0))
    return pl.pallas_call(
        functools.partial(_ffn_kernel, alpha, 1), grid=(m // tm,),
        in_specs=[row, lw.spec("w_up"), lw.spec("w_down"), lw.spec("ln2_g"), lw.spec("ln2_b")],
        out_specs=row, out_shape=jax.ShapeDtypeStruct((m, d), F32),
        compiler_params=_params("parallel"), name="ffn",
    )(x, lw["w_up"], lw["w_down"], lw["ln2_g"], lw["ln2_b"])


def _rope_table(pos):
    half = MLA_D_ROPE // 2
    inv = 1.0 / (ROPE_THETA ** (jnp.arange(half, dtype=F32) / half))
    ang = pos.astype(F32)[:, None] * inv[None]
    cos, sin = jnp.cos(ang), jnp.sin(ang)
    z = jnp.zeros((pos.shape[0], LANES - MLA_D_ROPE), F32)
    zh = jnp.zeros_like(sin)
    return jnp.concatenate([cos, cos, z, zh, sin, z, -sin, zh, z], -1)


def _prep_weights(w_in, q_norm_g, w_uq, w_uk, kv_norm_g, w_uv, hg_lb, hg_norm_g, cm_ln_g, cm_ln_b,
                  cm_ws, cm_bs, w_pa, w_pb, w_pc, w_o, ln1_g, ln1_b, w_up, w_down, ln2_g, ln2_b):
    depth, d, _ = w_in.shape
    o_kr = 2 * MLA_RANK
    assert w_in.shape[2] - o_kr - MLA_D_ROPE == _B_END
    w_a = w_in[:, :, :o_kr].astype(BF16)
    w_b = w_in[:, :, o_kr + MLA_D_ROPE:].astype(BF16)
    kr = w_in[:, :, o_kr:o_kr + MLA_D_ROPE]
    zk = jnp.zeros((depth, d, LANES - MLA_D_ROPE), F32)
    w_kr = jnp.concatenate([kr, zk], -1).astype(BF16)

    dq = MLA_D_NOPE + MLA_D_ROPE
    uq = w_uq.reshape(depth, MLA_RANK, MLA_HEADS, dq)
    nope, rot = uq[..., :MLA_D_NOPE], uq[..., MLA_D_NOPE:]
    zn = jnp.zeros((depth, MLA_RANK, MLA_HEADS, LANES - MLA_D_NOPE), F32)
    zr = jnp.zeros((depth, MLA_RANK, MLA_HEADS, LANES - MLA_D_ROPE), F32)
    flat = lambda t: t.reshape(depth, MLA_RANK, MLA_HEADS * LANES)
    w_uq_r = jnp.concatenate([flat(jnp.concatenate([nope, zn], -1)),
                              flat(jnp.concatenate([rot, zr], -1))], -1).astype(BF16)
    ukt = jnp.transpose(w_uk, (0, 2, 3, 1))
    w_uk_r = jnp.concatenate([ukt, jnp.zeros((depth, MLA_HEADS, LANES - MLA_D_NOPE, MLA_RANK), F32)], 2).astype(BF16)
    uvt = jnp.transpose(w_uv, (0, 2, 1, 3)).astype(BF16)
    lane_half = (jnp.arange(2 * MLA_D_V) // MLA_D_V)[None, None, None, :]
    head_half = (jnp.arange(MLA_HEADS) % 2)[None, :, None, None]
    w_uvp = jnp.where(lane_half == head_half, jnp.concatenate([uvt, uvt], -1), jnp.zeros((), BF16))

    sm = jax.nn.softmax(hg_lb.astype(F32), axis=0)
    lb = jnp.concatenate([jnp.zeros_like(sm[:1]), jnp.cumsum(sm[1:], axis=0)], axis=0)
    lbp = jnp.stack([jnp.log(lb + LB_TINY), jnp.log1p(-lb), 1.0 - lb], 1)

    bs_full = jnp.repeat(jnp.transpose(cm_bs, (0, 2, 1)), CM_GROUP_DIM, axis=-1)

    r3 = lambda t: t.reshape(depth, 1, -1)
    return dict(w_a=w_a, w_kr=w_kr, w_b=w_b, w_uq=w_uq_r, w_uk=w_uk_r, w_uvp=w_uvp, lbp=lbp,
                q_norm_g=r3(q_norm_g), kv_norm_g=r3(kv_norm_g), hg_norm_g=r3(hg_norm_g),
                cm_ln_g=r3(cm_ln_g), cm_ln_b=r3(cm_ln_b), cm_ws=cm_ws, cm_bs_full=bs_full,
                w_pa=w_pa.astype(BF16), w_pb=w_pb.astype(BF16), w_pc=w_pc.astype(BF16), w_o=w_o.astype(BF16),
                ln1_g=r3(ln1_g), ln1_b=r3(ln1_b), w_up=w_up.astype(BF16), w_down=w_down.astype(BF16),
                ln2_g=r3(ln2_g), ln2_b=r3(ln2_b))


def _hgrn_rows(batch):
    for nb in (8, 4, 2, 1):
        if batch % nb == 0:
            return nb


def _row_tile(m, want):
    tm = min(m, want)
    assert m % tm == 0, (m, tm)
    return tm


def _layer(x, lw, rope_tab, alpha, batch, seq, past, prev, emb, tm):
    prompt = past is None
    outs = _proj(x, lw, rope_tab, tm, BF16 if prompt else F32, prev, emb)
    if emb is not None:
        x = outs[-1]
    q, k, ckv, kr, hq, hk, hlf, hv, hgt, u, vcm, sga, sgb, sgc = outs[:14]
    if prompt:
        a_in = _attn_prompt(q, k, lw, batch, seq, Q_BLOCK)
        hg_o, s_fin = _hgrn(hq, hk, hlf, hv, hgt, lw, None, batch, seq, CHUNK, _hgrn_rows(batch))
    else:
        a_in = _attn_sample(q, k, past[0], past[1], lw, batch, seq)
        hg_o, s_fin = _hgrn(hq, hk, hlf, hv, hgt, lw, past[2], batch, seq, seq, _hgrn_rows(batch))
    cml = min(seq, CM_CHUNK)
    tm_wide = _row_tile(x.shape[0], WIDE_ROW_TILE)
    x = _merge(x, a_in, hg_o, u, vcm, sga, sgb, sgc, lw, alpha, cml, tm_wide)
    x = _ffn(x, lw, alpha, tm_wide)
    return x, (ckv, kr), (s_fin, vcm)


def kernel(x_prompt, x_sample, cache_mla_ckv, cache_mla_krope, state_hgrn, emb_ln_g, emb_ln_b, w_in, q_norm_g, w_uq, w_uk, kv_norm_g, w_uv, hg_lb, hg_norm_g, cm_ln_g, cm_ln_b, cm_ws, cm_bs, w_pa, w_pb, w_pc, w_o, ln1_g, ln1_b, w_up, w_down, ln2_g, ln2_b):
    bp, sp, d = x_prompt.shape
    bs, ss, _ = x_sample.shape
    depth = w_in.shape[0]
    past_len = cache_mla_ckv.shape[2]
    assert sp % CM_CHUNK == 0 and sp % KV_BLOCK == 0 and ss <= CHUNK and ss % 16 == 0
    alpha = float((2 * depth) ** 0.25)

    wts = _prep_weights(w_in, q_norm_g, w_uq, w_uk, kv_norm_g, w_uv, hg_lb, hg_norm_g, cm_ln_g, cm_ln_b,
                        cm_ws, cm_bs, w_pa, w_pb, w_pc, w_o, ln1_g, ln1_b, w_up, w_down, ln2_g, ln2_b)
    tm_p = _row_tile(bp * sp, ROW_TILE)
    tm_s = _row_tile(bs * ss, ROW_TILE)
    assert (sp % tm_p == 0 or tm_p % sp == 0) and (ss % tm_s == 0 or tm_s % ss == 0)
    rope_p = jnp.tile(_rope_table(jnp.arange(sp, dtype=jnp.int32)), (max(1, tm_p // sp), 1))
    rope_s = jnp.tile(_rope_table(past_len + jnp.arange(ss, dtype=jnp.int32)), (max(1, tm_s // ss), 1))
    emb = (emb_ln_g.reshape(1, d), emb_ln_b.reshape(1, d))
    xp = x_prompt.reshape(bp * sp, d)
    xs = x_sample.reshape(bs * ss, d)
    past = (cache_mla_ckv, jnp.swapaxes(cache_mla_krope, -1, -2), state_hgrn)

    stacked = lambda m: (jnp.zeros((depth, m, MLA_RANK), F32), jnp.zeros((depth, m, MLA_D_ROPE), F32))
    kv_p, kv_s = stacked(bp * sp), stacked(bs * ss)
    st_p, st_s, v_s = [], [], []
    for l in range(depth):
        lw = _LayerWeights(wts, l)
        xp, kv_p, (s_p, _) = _layer(xp, lw, rope_p, alpha, bp, sp, None, kv_p, emb if l == 0 else None, tm_p)
        xs, kv_s, (s_s, v) = _layer(xs, lw, rope_s, alpha, bs, ss, past, kv_s, emb if l == 0 else None, tm_s)
        st_p.append(s_p)
        st_s.append(s_s)
        v_s.append(v.reshape(bs, ss, -1))
    return (xp.reshape(bp, sp, d), xs.reshape(bs, ss, d),
            kv_p[0].reshape(depth, bp, sp, -1), kv_p[1].reshape(depth, bp, sp, -1), jnp.stack(st_p),
            kv_s[0].reshape(depth, bs, ss, -1), kv_s[1].reshape(depth, bs, ss, -1), jnp.stack(st_s),
            jnp.stack(v_s))
```

```python
import functools

import numpy as np
import jax
import jax.numpy as jnp
from jax import lax
from jax.experimental import pallas as pl
from jax.experimental.pallas import tpu as pltpu

F32 = jnp.float32
BF16 = jnp.bfloat16

CHUNK = 64
MLA_HEADS = 8
MLA_D_NOPE = 64
MLA_D_ROPE = 32
MLA_D_V = 64
MLA_RANK = 256
MLA_SCALE = (MLA_D_NOPE + MLA_D_ROPE) ** -0.5
LOG2E = float(np.log2(np.e))
Q_SCALE = MLA_SCALE * LOG2E
ROPE_THETA = 10000.0
MASK_VALUE = -1e30
HG_HEADS = 4
HG_DK = 128
HG_DV = 128
HG_W = HG_HEADS * HG_DK
LB_TINY = 1e-30
LOGF_SPLIT = 3
CM_CHUNK = 128
CM_GROUPS = 4
CM_WIDTH = 512
CM_GROUP_DIM = CM_WIDTH // CM_GROUPS
EPS = 1e-5

LANES = 128
VMEM_LIMIT_BYTES = 56 * 1024 * 1024
QK_WIDTH = MLA_RANK + LANES
KV_BLOCK = 256
Q_BLOCK = 256
ROW_TILE = 512
WIDE_ROW_TILE = 512

_A_CQ, _A_CKV, _A_END = 0, 256, 512
_B_HQ, _B_HF, _B_HI, _B_HG, _B_CU, _B_CV, _B_GA, _B_GB, _B_GC, _B_END = (
    0, 512, 1024, 1536, 2048, 2560, 3072, 4096, 5120, 6144)


def _const_spec(shape):
    nd = len(shape)
    return pl.BlockSpec(shape, lambda *_: (0,) * nd, pipeline_mode=pl.Buffered(1))


class _LayerWeights:
    def __init__(self, stacked, layer):
        self.stacked = stacked
        self.layer = layer

    def __getitem__(self, name):
        return self.stacked[name]

    def spec(self, name):
        shape = self.stacked[name].shape[1:]
        layer, nd = self.layer, len(shape)
        return pl.BlockSpec((None,) + shape, lambda *_: (layer,) + (0,) * nd, pipeline_mode=pl.Buffered(1))


def _params(*sem):
    return pltpu.CompilerParams(dimension_semantics=sem, vmem_limit_bytes=VMEM_LIMIT_BYTES)


def _layer_norm(x, g, b):
    mu = jnp.mean(x, -1, keepdims=True)
    xc = x - mu
    var = jnp.mean(xc * xc, -1, keepdims=True)
    return xc * lax.rsqrt(var + EPS) * g + b


def _rms_norm(x, g):
    return x * lax.rsqrt(jnp.mean(x * x, -1, keepdims=True) + EPS) * g


def _gelu(x):
    return 0.5 * x * (1.0 + lax.erf(x * np.float32(1.0 / np.sqrt(2.0))))


def _sigmoid(x):
    return 1.0 / (1.0 + jnp.exp(-x))


def _silu(x):
    return x * _sigmoid(x)


def _log_sigmoid(x):
    return jnp.minimum(x, 0.0) - jnp.log(1.0 + jnp.exp(-jnp.abs(x)))


def _proj_kernel(n_alias, pre_ln, x_ref, wa_ref, wkr_ref, wb_ref, wuq_ref, wuk_ref, qg_ref, kvg_ref, rope_ref,
                 lbp_ref, cmg_ref, cmb_ref, *rest):
    (q_ref, kv_ref, ckv_ref, kr_ref, hq_ref, hk_ref, hlf_ref, hv_ref, hgt_ref,
     u_ref, vcm_ref, sga_ref, sgb_ref, sgc_ref) = rest[n_alias + 2 * pre_ln:][:14]
    x = x_ref[...]
    if pre_ln:
        x = _layer_norm(x, rest[0][...], rest[1][...])
        rest[-1][...] = x
    xb = x.astype(BF16)

    def proj_a(lo, hi):
        return jnp.dot(xb, wa_ref[:, lo:hi], preferred_element_type=F32)

    def proj(lo, hi):
        return jnp.dot(xb, wb_ref[:, lo:hi], preferred_element_type=F32)

    cos_t = rope_ref[:, :LANES]
    sin_up = rope_ref[:, LANES:2 * LANES]
    sin_dn = rope_ref[:, 2 * LANES:]
    half = MLA_D_ROPE // 2

    def rotary(t):
        return (t * cos_t + pltpu.roll(t, half, axis=1) * sin_up
                + pltpu.roll(t, LANES - half, axis=1) * sin_dn)

    cqn = _rms_norm(proj_a(_A_CQ, _A_CKV), qg_ref[...]).astype(BF16)
    q3 = jnp.dot(cqn, wuq_ref[...], preferred_element_type=F32)
    hw = MLA_HEADS * LANES
    for h in range(MLA_HEADS):
        nope = q3[:, h * LANES:(h + 1) * LANES].astype(BF16)
        lat = jnp.dot(nope, wuk_ref[h], preferred_element_type=F32) * Q_SCALE
        rot = rotary(q3[:, hw + h * LANES:hw + (h + 1) * LANES]) * Q_SCALE
        q_ref[h, :, :MLA_RANK] = lat.astype(BF16)
        q_ref[h, :, MLA_RANK:] = rot.astype(BF16)

    ckv = _rms_norm(proj_a(_A_CKV, _A_END), kvg_ref[...])
    ckv_ref[...] = ckv
    kv_ref[:, :MLA_RANK] = ckv.astype(BF16)
    zk = jnp.dot(xb, wkr_ref[...], preferred_element_type=F32)
    krot = rotary(zk)
    kr_ref[...] = krot[:, :MLA_D_ROPE]
    kv_ref[:, MLA_RANK:] = krot.astype(BF16)

    hq_ref[...] = proj(_B_HQ, _B_HF)
    zf = proj(_B_HF, _B_HI)
    log_lb = lbp_ref[0:1, :]
    log_1m = lbp_ref[1:2, :]
    one_m = lbp_ref[2:3, :]
    c = log_1m + _log_sigmoid(zf)
    hlf_ref[...] = jnp.maximum(log_lb, c) + jnp.log(1.0 + jnp.exp(-jnp.abs(log_lb - c)))
    hk_ref[...] = one_m * _sigmoid(-zf)
    hv_ref[...] = proj(_B_HI, _B_HG).astype(BF16)
    hgt_ref[...] = _silu(proj(_B_HG, _B_CU)).astype(BF16)

    u_ref[...] = _gelu(proj(_B_CU, _B_CV)).astype(BF16)
    vcm_ref[...] = _layer_norm(_gelu(proj(_B_CV, _B_GA)), cmg_ref[...], cmb_ref[...]).astype(vcm_ref.dtype)

    sga_ref[...] = _sigmoid(proj(_B_GA, _B_GB)).astype(BF16)
    sgb_ref[...] = _sigmoid(proj(_B_GB, _B_GC)).astype(BF16)
    sgc_ref[...] = _sigmoid(proj(_B_GC, _B_END)).astype(BF16)


def _proj(x, lw, rope_tab, tm, vcm_dtype, prev, emb):
    m, d = x.shape
    depth = lw["w_b"].shape[0]
    layer = lw.layer
    row = lambda w: pl.BlockSpec((tm, w), lambda i: (i, 0))
    lrow = lambda w: pl.BlockSpec((None, tm, w), lambda i: (layer, i, 0))
    out_shape = [
        jax.ShapeDtypeStruct((MLA_HEADS, m, QK_WIDTH), BF16),
        jax.ShapeDtypeStruct((m, QK_WIDTH), BF16),
        jax.ShapeDtypeStruct((depth, m, MLA_RANK), F32),
        jax.ShapeDtypeStruct((depth, m, MLA_D_ROPE), F32),
        jax.ShapeDtypeStruct((m, HG_W), F32),
        jax.ShapeDtypeStruct((m, HG_W), F32),
        jax.ShapeDtypeStruct((m, HG_W), F32),
        jax.ShapeDtypeStruct((m, HG_W), BF16),
        jax.ShapeDtypeStruct((m, HG_W), BF16),
        jax.ShapeDtypeStruct((m, CM_WIDTH), BF16),
        jax.ShapeDtypeStruct((m, CM_WIDTH), vcm_dtype),
        jax.ShapeDtypeStruct((m, d), BF16),
        jax.ShapeDtypeStruct((m, d), BF16),
        jax.ShapeDtypeStruct((m, d), BF16),
    ]
    out_specs = [
        pl.BlockSpec((MLA_HEADS, tm, QK_WIDTH), lambda i: (0, i, 0)),
        row(QK_WIDTH), lrow(MLA_RANK), lrow(MLA_D_ROPE),
        row(HG_W), row(HG_W), row(HG_W), row(HG_W), row(HG_W),
        row(CM_WIDTH), row(CM_WIDTH), row(d), row(d), row(d),
    ]
    names = ("w_a", "w_kr", "w_b", "w_uq", "w_uk", "q_norm_g", "kv_norm_g")
    tail = ("lbp", "cm_ln_g", "cm_ln_b")
    in_specs = ([row(d)] + [lw.spec(n) for n in names]
                + [pl.BlockSpec((tm, 3 * LANES), lambda i: (i % (rope_tab.shape[0] // tm), 0))]
                + [lw.spec(n) for n in tail])
    args = [x] + [lw[n] for n in names] + [rope_tab] + [lw[n] for n in tail]
    if emb is not None:
        in_specs += [_const_spec((1, d))] * 2
        args += list(emb)
        out_shape.append(jax.ShapeDtypeStruct((m, d), F32))
        out_specs.append(row(d))
    aliases = {len(args): 2, len(args) + 1: 3}
    in_specs += [pl.BlockSpec(memory_space=pl.ANY)] * 2
    args += list(prev)
    return pl.pallas_call(
        functools.partial(_proj_kernel, len(aliases), emb is not None), grid=(m // tm,), in_specs=in_specs,
        out_specs=out_specs, out_shape=out_shape, input_output_aliases=aliases,
        compiler_params=_params("parallel"), name="proj",
    )(*args)


def _uv_project(o, wuv_ref, o_ref, tq):
    ob = o.astype(BF16)
    for i in range(MLA_HEADS // 2):
        pair = (jnp.dot(ob[2 * i * tq:(2 * i + 1) * tq], wuv_ref[2 * i], preferred_element_type=F32)
                + jnp.dot(ob[(2 * i + 1) * tq:(2 * i + 2) * tq], wuv_ref[2 * i + 1], preferred_element_type=F32))
        o_ref[:, i * LANES:(i + 1) * LANES] = pair.astype(BF16)


def _attn_prompt_kernel(q_ref, k_ref, wuv_ref, o_ref, m_ref, l_ref, acc_ref):
    qb = pl.program_id(1)
    tq = q_ref.shape[1]
    rows = MLA_HEADS * tq
    q = q_ref[...].reshape(rows, QK_WIDTH)
    kv_len = (qb + 1) * tq
    nblk = (kv_len + KV_BLOCK - 1) // KV_BLOCK
    rep = KV_BLOCK // LANES

    def scores(j):
        start = pl.multiple_of(j * KV_BLOCK, KV_BLOCK)
        return lax.dot_general(q, k_ref[pl.ds(start, KV_BLOCK), :], (((1,), (1,)), ((), ())),
                               preferred_element_type=F32)

    def values(j):
        start = pl.multiple_of(j * KV_BLOCK, KV_BLOCK)
        return k_ref[pl.ds(start, KV_BLOCK), :MLA_RANK]

    def lane_sums(p):
        return sum(p[:, i * LANES:(i + 1) * LANES] for i in range(rep))

    def update(j, s):
        m_prev = m_ref[...]
        m_new = jnp.maximum(m_prev, jnp.max(s, -1, keepdims=True))
        alpha = jnp.exp2(m_prev - m_new)
        p = jnp.exp2(s - jnp.tile(m_new, (1, rep)))
        l_ref[...] = alpha * l_ref[...] + lane_sums(p)
        acc_ref[...] = jnp.tile(alpha, (1, MLA_RANK // LANES)) * acc_ref[...] + jnp.dot(
            p.astype(BF16), values(j), preferred_element_type=F32)
        m_ref[...] = m_new

    s_diag = scores(nblk - 1)
    s_first = scores(0)
    lane = lax.broadcasted_iota(jnp.int32, (CHUNK, KV_BLOCK), 1)
    visible = [lane < qb * tq + (c + 1) * CHUNK - (nblk - 1) * KV_BLOCK for c in range(tq // CHUNK)]
    s_diag = jnp.concatenate(
        [jnp.where(visible[c], s_diag[h * tq + c * CHUNK:h * tq + (c + 1) * CHUNK], MASK_VALUE)
         for h in range(MLA_HEADS) for c in range(tq // CHUNK)], 0)
    m_diag = jnp.broadcast_to(jnp.max(s_diag, -1, keepdims=True), m_ref.shape)
    p_diag = jnp.exp2(s_diag - jnp.tile(m_diag, (1, rep)))
    m_ref[...] = m_diag
    l_ref[...] = lane_sums(p_diag)
    acc_ref[...] = jnp.dot(p_diag.astype(BF16), values(nblk - 1), preferred_element_type=F32)

    def body(j, s):
        s_next = scores(j + 1)
        update(j, s)
        return s_next

    s_last = lax.fori_loop(0, nblk - 2, body, s_first)

    @pl.when(nblk >= 2)
    def _():
        update(nblk - 2, s_last)

    l_tot = jnp.broadcast_to(jnp.sum(l_ref[...], -1, keepdims=True), l_ref.shape)
    o = acc_ref[...] * jnp.tile(1.0 / l_tot, (1, MLA_RANK // LANES))
    _uv_project(o, wuv_ref, o_ref, tq)


def _attn_prompt(q, k, lw, batch, seq, tq):
    m = batch * seq
    nqc = seq // tq
    rows = MLA_HEADS * tq
    assert tq & (tq - 1) == 0 and tq % CHUNK == 0 and seq % tq == 0
    return pl.pallas_call(
        _attn_prompt_kernel, grid=(batch, nqc),
        in_specs=[pl.BlockSpec((MLA_HEADS, tq, QK_WIDTH), lambda b, c: (0, b * nqc + c, 0)),
                  pl.BlockSpec((seq, QK_WIDTH), lambda b, c: (b, 0)), lw.spec("w_uvp")],
        out_specs=pl.BlockSpec((tq, MLA_HEADS * MLA_D_V), lambda b, c: (b * nqc + c, 0)),
        out_shape=jax.ShapeDtypeStruct((m, MLA_HEADS * MLA_D_V), BF16),
        scratch_shapes=[pltpu.VMEM((rows, LANES), F32), pltpu.VMEM((rows, LANES), F32),
                        pltpu.VMEM((rows, MLA_RANK), F32)],
        compiler_params=_params("parallel", "arbitrary"), name="attn_prompt",
    )(q, k, lw["w_uvp"])


def _attn_sample_kernel(past_len, q_ref, k_ref, cckv_ref, ckrt_ref, wuv_ref, o_ref):
    nq = q_ref.shape[1]
    rows = MLA_HEADS * nq
    q = q_ref[...].reshape(rows, QK_WIDTH)
    q_lat = q[:, :MLA_RANK]
    q_rot = q[:, MLA_RANK:MLA_RANK + MLA_D_ROPE]
    nt = (((1,), (1,)), ((), ()))
    kc = cckv_ref[...].astype(BF16)
    s1 = (lax.dot_general(q_lat, kc, nt, preferred_element_type=F32)
          + jnp.dot(q_rot, ckrt_ref[...].astype(BF16), preferred_element_type=F32))
    kn = k_ref[...]
    s2 = lax.dot_general(q, kn, nt, preferred_element_type=F32)
    q_pos = past_len + lax.broadcasted_iota(jnp.int32, s2.shape, 0) % nq
    k_pos = past_len + lax.broadcasted_iota(jnp.int32, s2.shape, 1)
    s2 = jnp.where(k_pos // CHUNK <= q_pos // CHUNK, s2, MASK_VALUE)
    mx = jnp.maximum(jnp.max(s1, -1, keepdims=True), jnp.max(s2, -1, keepdims=True))
    p1 = jnp.exp2(s1 - mx)
    p2 = jnp.exp2(s2 - mx)
    l = jnp.sum(p1, -1, keepdims=True) + jnp.sum(p2, -1, keepdims=True)
    o = (jnp.dot(p1.astype(BF16), kc, preferred_element_type=F32)
         + jnp.dot(p2.astype(BF16), kn[:, :MLA_RANK], preferred_element_type=F32)) * (1.0 / l)
    _uv_project(o, wuv_ref, o_ref, nq)


def _attn_sample(q, k, cache_ckv, cache_krt, lw, batch, nq):
    past_len = cache_ckv.shape[2]
    layer = lw.layer
    return pl.pallas_call(
        functools.partial(_attn_sample_kernel, past_len), grid=(batch,),
        in_specs=[pl.BlockSpec((MLA_HEADS, nq, QK_WIDTH), lambda b: (0, b, 0)),
                  pl.BlockSpec((nq, QK_WIDTH), lambda b: (b, 0)),
                  pl.BlockSpec((None, None, past_len, MLA_RANK), lambda b: (layer, b, 0, 0)),
                  pl.BlockSpec((None, None, MLA_D_ROPE, past_len), lambda b: (layer, b, 0, 0)),
                  lw.spec("w_uvp")],
        out_specs=pl.BlockSpec((nq, MLA_HEADS * MLA_D_V), lambda b: (b, 0)),
        out_shape=jax.ShapeDtypeStruct((batch * nq, MLA_HEADS * MLA_D_V), BF16),
        compiler_params=_params("parallel"), name="attn_sample",
    )(q, k, cache_ckv, cache_krt, lw["w_uvp"])


def _hgrn_levels(lc):
    spans = []
    sp = lc // 2
    while sp >= 1:
        spans.append(sp)
        sp //= 2
    return spans


def _hgrn_consts(lc):
    spans = _hgrn_levels(lc)
    tri = np.tril(np.ones((lc, lc), np.float32))
    t = np.arange(lc)
    mats, masks = [tri], []
    for sp in spans:
        blk = t // (2 * sp)
        ref_row = blk * 2 * sp + sp - 1
        mats.append(np.abs(tri - tri[ref_row]))
        right = (t % (2 * sp)) >= sp
        masks.append(((blk[:, None] == blk[None, :]) & right[:, None] & ~right[None, :]).astype(np.float32))
    return np.tile(np.concatenate(mats, 0), (1, LOGF_SPLIT)), np.stack(masks, 0)


def _hgrn_kernel(has_init, q_ref, k_ref, lf_ref, v_ref, gt_ref, gn_ref, cmat_ref, mask_ref, *rest):
    s0_ref = rest[0] if has_init else None
    o_ref, sout_ref, st_ref = rest[-3:]
    c = pl.program_id(1)
    nb, lc, _ = q_ref.shape
    nlev = mask_ref.shape[0]
    nt = (((1,), (1,)), ((), ()))
    tn = (((0,), (0,)), ((), ()))

    @pl.when(c == 0)
    def _():
        for bi in range(nb):
            for h in range(HG_HEADS):
                if has_init:
                    st_ref[bi, h] = s0_ref[bi, h].T
                else:
                    st_ref[bi, h] = jnp.zeros((HG_DV, HG_DK), F32)

    eye = (lax.broadcasted_iota(jnp.int32, (lc, lc), 0) == lax.broadcasted_iota(jnp.int32, (lc, lc), 1))
    lvl_mask = [mask_ref[lv] > 0.5 for lv in range(nlev)]
    for bi in range(nb):
        lf = lf_ref[bi] * LOG2E
        parts, rem = [], lf
        for _ in range(LOGF_SPLIT):
            parts.append(rem.astype(BF16))
            rem = rem - parts[-1].astype(F32)
        br = jnp.dot(cmat_ref[...], jnp.concatenate(parts, 0), preferred_element_type=F32)
        for h in range(HG_HEADS):
            sl = slice(h * HG_DK, (h + 1) * HG_DK)
            bh = br[:lc, sl]
            qh = q_ref[bi, :, sl]
            kh = k_ref[bi, :, sl]
            vh = v_ref[bi, :, sl]
            a = jnp.where(eye, jnp.sum(qh * kh, -1, keepdims=True), 0.0)
            for lv in range(nlev):
                e = jnp.exp2(br[(1 + lv) * lc:(2 + lv) * lc, sl])
                a = jnp.where(lvl_mask[lv],
                              lax.dot_general((qh * e).astype(BF16), (kh * e).astype(BF16), nt,
                                              preferred_element_type=F32), a)
            st = st_ref[bi, h]
            o = (jnp.dot(a.astype(BF16), vh, preferred_element_type=F32)
                 + lax.dot_general((qh * jnp.exp2(bh)).astype(BF16), st.astype(BF16), nt,
                                   preferred_element_type=F32))
            b_last = bh[lc - 1:lc, :]
            kdec = (kh * jnp.exp2(b_last - bh)).astype(BF16)
            st_ref[bi, h] = st * jnp.exp2(b_last) + lax.dot_general(vh, kdec, tn, preferred_element_type=F32)
            on = _rms_norm(o, gn_ref[:, sl])
            o_ref[bi, :, sl] = (on * gt_ref[bi, :, sl].astype(F32)).astype(BF16)

    @pl.when(c == pl.num_programs(1) - 1)
    def _():
        for bi in range(nb):
            for h in range(HG_HEADS):
                sout_ref[bi, h] = st_ref[bi, h].T


def _hgrn(hq, hk, hlf, hv, hgt, lw, state, batch, seq, lc, nb):
    nch = seq // lc
    assert batch % nb == 0
    cmat, masks = _hgrn_consts(lc)
    has_init = state is not None
    layer = lw.layer
    row = pl.BlockSpec((nb, lc, HG_W), lambda b, c: (b, c, 0))
    st_spec = pl.BlockSpec((nb, HG_HEADS, HG_DK, HG_DV), lambda b, c: (b, 0, 0, 0))
    in_specs = [row, row, row, row, row, lw.spec("hg_norm_g"), _const_spec(cmat.shape), _const_spec(masks.shape)]
    args = [t.reshape(batch, seq, HG_W) for t in (hq, hk, hlf, hv, hgt)]
    args += [lw["hg_norm_g"], jnp.asarray(cmat, BF16), jnp.asarray(masks)]
    if has_init:
        in_specs.append(pl.BlockSpec((None, nb, HG_HEADS, HG_DK, HG_DV), lambda b, c: (layer, b, 0, 0, 0)))
        args.append(state)
    o, s_fin = pl.pallas_call(
        functools.partial(_hgrn_kernel, has_init), grid=(batch // nb, nch),
        in_specs=in_specs, out_specs=[row, st_spec],
        out_shape=[jax.ShapeDtypeStruct((batch, seq, HG_W), BF16),
                   jax.ShapeDtypeStruct((batch, HG_HEADS, HG_DK, HG_DV), F32)],
        scratch_shapes=[pltpu.VMEM((nb, HG_HEADS, HG_DV, HG_DK), F32)],
        compiler_params=_params("parallel", "arbitrary"), name="hgrn",
    )(*args)
    return o.reshape(batch * seq, HG_W), s_fin


def _merge_kernel(alpha, cml, x_ref, ain_ref, hg_ref, u_ref, vcm_ref, sga_ref, sgb_ref, sgc_ref,
                  wpa_ref, wpb_ref, wpc_ref, wo_ref, ws_ref, bs_ref, g_ref, b_ref,
                  o_ref, cin_ref):
    tm = x_ref.shape[0]
    tril = (lax.broadcasted_iota(jnp.int32, (cml, cml), 0) >= lax.broadcasted_iota(jnp.int32, (cml, cml), 1))
    for g in range(CM_GROUPS):
        sl = slice(g * CM_GROUP_DIM, (g + 1) * CM_GROUP_DIM)
        wg = jnp.where(tril, ws_ref[g, :cml, :cml], 0.0).astype(BF16)
        for r in range(tm // cml):
            rs = slice(r * cml, (r + 1) * cml)
            s = jnp.dot(wg, vcm_ref[rs, sl].astype(BF16), preferred_element_type=F32) + bs_ref[:cml, sl]
            cin_ref[rs, sl] = (u_ref[rs, sl].astype(F32) * s).astype(BF16)
    y_a = jnp.dot(ain_ref[...], wpa_ref[...], preferred_element_type=F32)
    y_b = jnp.dot(hg_ref[...], wpb_ref[...], preferred_element_type=F32)
    y_c = jnp.dot(cin_ref[...], wpc_ref[...], preferred_element_type=F32)
    mrg = (sga_ref[...].astype(F32) * y_a + sgb_ref[...].astype(F32) * y_b
           + sgc_ref[...].astype(F32) * y_c).astype(BF16)
    y = alpha * x_ref[...] + jnp.dot(mrg, wo_ref[...], preferred_element_type=F32)
    o_ref[...] = _layer_norm(y, g_ref[...], b_ref[...])


def _merge(x, a_in, hg_o, u, vcm, sga, sgb, sgc, lw, alpha, cml, tm):
    m, d = x.shape
    row = lambda w: pl.BlockSpec((tm, w), lambda i: (i, 0))
    in_specs = [row(d), row(MLA_HEADS * MLA_D_V), row(HG_W), row(CM_WIDTH), row(CM_WIDTH), row(d), row(d), row(d),
                lw.spec("w_pa"), lw.spec("w_pb"), lw.spec("w_pc"), lw.spec("w_o"),
                lw.spec("cm_ws"), lw.spec("cm_bs_full"), lw.spec("ln1_g"), lw.spec("ln1_b")]
    return pl.pallas_call(
        functools.partial(_merge_kernel, alpha, cml), grid=(m // tm,), in_specs=in_specs,
        out_specs=row(d), out_shape=jax.ShapeDtypeStruct((m, d), F32),
        scratch_shapes=[pltpu.VMEM((tm, CM_WIDTH), BF16)],
        compiler_params=_params("parallel"), name="merge",
    )(x, a_in, hg_o, u, vcm, sga, sgb, sgc, lw["w_pa"], lw["w_pb"], lw["w_pc"], lw["w_o"],
      lw["cm_ws"], lw["cm_bs_full"], lw["ln1_g"], lw["ln1_b"])


def _ffn_kernel(alpha, nsplit, x_ref, wup_ref, wdn_ref, g_ref, b_ref, o_ref):
    x = x_ref[...]
    xb = x.astype(BF16)
    dff = wdn_ref.shape[0]
    cw = dff // nsplit
    y = alpha * x
    for j in range(nsplit):
        gate = jnp.dot(xb, wup_ref[:, j * cw:(j + 1) * cw], preferred_element_type=F32)
        up = jnp.dot(xb, wup_ref[:, dff + j * cw:dff + (j + 1) * cw], preferred_element_type=F32)
        act = (_silu(gate) * up).astype(BF16)
        y = y + jnp.dot(act, wdn_ref[j * cw:(j + 1) * cw, :], preferred_element_type=F32)
    o_ref[...] = _layer_norm(y, g_ref[...], b_ref[...])


def _ffn(x, lw, alpha, tm):
    m, d = x.shape
    row = pl.BlockSpec((tm, d), lambda i: (i, 0))
    return pl.pallas_call(
        functools.partial(_ffn_kernel, alpha, 1), grid=(m // tm,),
        in_specs=[row, lw.spec("w_up"), lw.spec("w_down"), lw.spec("ln2_g"), lw.spec("ln2_b")],
        out_specs=row, out_shape=jax.ShapeDtypeStruct((m, d), F32),
        compiler_params=_params("parallel"), name="ffn",
    )(x, lw["w_up"], lw["w_down"], lw["ln2_g"], lw["ln2_b"])


def _rope_table(pos):
    half = MLA_D_ROPE // 2
    inv = 1.0 / (ROPE_THETA ** (jnp.arange(half, dtype=F32) / half))
    ang = pos.astype(F32)[:, None] * inv[None]
    cos, sin = jnp.cos(ang), jnp.sin(ang)
    z = jnp.zeros((pos.shape[0], LANES - MLA_D_ROPE), F32)
    zh = jnp.zeros_like(sin)
    return jnp.concatenate([cos, cos, z, zh, sin, z, -sin, zh, z], -1)


def _prep_weights(w_in, q_norm_g, w_uq, w_uk, kv_norm_g, w_uv, hg_lb, hg_norm_g, cm_ln_g, cm_ln_b,
                  cm_ws, cm_bs, w_pa, w_pb, w_pc, w_o, ln1_g, ln1_b, w_up, w_down, ln2_g, ln2_b):
    depth, d, _ = w_in.shape
    o_kr = 2 * MLA_RANK
    assert w_in.shape[2] - o_kr - MLA_D_ROPE == _B_END
    w_a = w_in[:, :, :o_kr].astype(BF16)
    w_b = w_in[:, :, o_kr + MLA_D_ROPE:].astype(BF16)
    kr = w_in[:, :, o_kr:o_kr + MLA_D_ROPE]
    zk = jnp.zeros((depth, d, LANES - MLA_D_ROPE), F32)
    w_kr = jnp.concatenate([kr, zk], -1).astype(BF16)

    dq = MLA_D_NOPE + MLA_D_ROPE
    uq = w_uq.reshape(depth, MLA_RANK, MLA_HEADS, dq)
    nope, rot = uq[..., :MLA_D_NOPE], uq[..., MLA_D_NOPE:]
    zn = jnp.zeros((depth, MLA_RANK, MLA_HEADS, LANES - MLA_D_NOPE), F32)
    zr = jnp.zeros((depth, MLA_RANK, MLA_HEADS, LANES - MLA_D_ROPE), F32)
    flat = lambda t: t.reshape(depth, MLA_RANK, MLA_HEADS * LANES)
    w_uq_r = jnp.concatenate([flat(jnp.concatenate([nope, zn], -1)),
                              flat(jnp.concatenate([rot, zr], -1))], -1).astype(BF16)
    ukt = jnp.transpose(w_uk, (0, 2, 3, 1))
    w_uk_r = jnp.concatenate([ukt, jnp.zeros((depth, MLA_HEADS, LANES - MLA_D_NOPE, MLA_RANK), F32)], 2).astype(BF16)
    uvt = jnp.transpose(w_uv, (0, 2, 1, 3)).astype(BF16)
    lane_half = (jnp.arange(2 * MLA_D_V) // MLA_D_V)[None, None, None, :]
    head_half = (jnp.arange(MLA_HEADS) % 2)[None, :, None, None]
    w_uvp = jnp.where(lane_half == head_half, jnp.concatenate([uvt, uvt], -1), jnp.zeros((), BF16))

    sm = jax.nn.softmax(hg_lb.astype(F32), axis=0)
    lb = jnp.concatenate([jnp.zeros_like(sm[:1]), jnp.cumsum(sm[1:], axis=0)], axis=0)
    lbp = jnp.stack([jnp.log(lb + LB_TINY), jnp.log1p(-lb), 1.0 - lb], 1)

    bs_full = jnp.repeat(jnp.transpose(cm_bs, (0, 2, 1)), CM_GROUP_DIM, axis=-1)

    r3 = lambda t: t.reshape(depth, 1, -1)
    return dict(w_a=w_a, w_kr=w_kr, w_b=w_b, w_uq=w_uq_r, w_uk=w_uk_r, w_uvp=w_uvp, lbp=lbp,
                q_norm_g=r3(q_norm_g), kv_norm_g=r3(kv_norm_g), hg_norm_g=r3(hg_norm_g),
                cm_ln_g=r3(cm_ln_g), cm_ln_b=r3(cm_ln_b), cm_ws=cm_ws, cm_bs_full=bs_full,
                w_pa=w_pa.astype(BF16), w_pb=w_pb.astype(BF16), w_pc=w_pc.astype(BF16), w_o=w_o.astype(BF16),
                ln1_g=r3(ln1_g), ln1_b=r3(ln1_b), w_up=w_up.astype(BF16), w_down=w_down.astype(BF16),
                ln2_g=r3(ln2_g), ln2_b=r3(ln2_b))


def _hgrn_rows(batch):
    for nb in (8, 4, 2, 1):
        if batch % nb == 0:
            return nb


def _row_tile(m, want):
    tm = min(m, want)
    assert m % tm == 0, (m, tm)
    return tm


def _layer(x, lw, rope_tab, alpha, batch, seq, past, prev, emb, tm):
    prompt = past is None
    outs = _proj(x, lw, rope_tab, tm, BF16 if prompt else F32, prev, emb)
    if emb is not None:
        x = outs[-1]
    q, k, ckv, kr, hq, hk, hlf, hv, hgt, u, vcm, sga, sgb, sgc = outs[:14]
    if prompt:
        a_in = _attn_prompt(q, k, lw, batch, seq, Q_BLOCK)
        hg_o, s_fin = _hgrn(hq, hk, hlf, hv, hgt, lw, None, batch, seq, CHUNK, _hgrn_rows(batch))
    else:
        a_in = _attn_sample(q, k, past[0], past[1], lw, batch, seq)
        hg_o, s_fin = _hgrn(hq, hk, hlf, hv, hgt, lw, past[2], batch, seq, seq, _hgrn_rows(batch))
    cml = min(seq, CM_CHUNK)
    tm_wide = _row_tile(x.shape[0], WIDE_ROW_TILE)
    x = _merge(x, a_in, hg_o, u, vcm, sga, sgb, sgc, lw, alpha, cml, tm_wide)
    x = _ffn(x, lw, alpha, tm_wide)
    return x, (ckv, kr), (s_fin, vcm)


def kernel(x_prompt, x_sample, cache_mla_ckv, cache_mla_krope, state_hgrn, emb_ln_g, emb_ln_b, w_in, q_norm_g, w_uq, w_uk, kv_norm_g, w_uv, hg_lb, hg_norm_g, cm_ln_g, cm_ln_b, cm_ws, cm_bs, w_pa, w_pb, w_pc, w_o, ln1_g, ln1_b, w_up, w_down, ln2_g, ln2_b):
    bp, sp, d = x_prompt.shape
    bs, ss, _ = x_sample.shape
    depth = w_in.shape[0]
    past_len = cache_mla_ckv.shape[2]
    assert sp % CM_CHUNK == 0 and sp % KV_BLOCK == 0 and ss <= CHUNK and ss % 16 == 0
    alpha = float((2 * depth) ** 0.25)

    wts = _prep_weights(w_in, q_norm_g, w_uq, w_uk, kv_norm_g, w_uv, hg_lb, hg_norm_g, cm_ln_g, cm_ln_b,
                        cm_ws, cm_bs, w_pa, w_pb, w_pc, w_o, ln1_g, ln1_b, w_up, w_down, ln2_g, ln2_b)
    tm_p = _row_tile(bp * sp, ROW_TILE)
    tm_s = _row_tile(bs * ss, ROW_TILE)
    assert (sp % tm_p == 0 or tm_p % sp == 0) and (ss % tm_s == 0 or tm_s % ss == 0)
    rope_p = jnp.tile(_rope_table(jnp.arange(sp, dtype=jnp.int32)), (max(1, tm_p // sp), 1))
    rope_s = jnp.tile(_rope_table(past_len + jnp.arange(ss, dtype=jnp.int32)), (max(1, tm_s // ss), 1))
    emb = (emb_ln_g.reshape(1, d), emb_ln_b.reshape(1, d))
    xp = x_prompt.reshape(bp * sp, d)
    xs = x_sample.reshape(bs * ss, d)
    past = (cache_mla_ckv, jnp.swapaxes(cache_mla_krope, -1, -2), state_hgrn)

    stacked = lambda m: (jnp.zeros((depth, m, MLA_RANK), F32), jnp.zeros((depth, m, MLA_D_ROPE), F32))
    kv_p, kv_s = stacked(bp * sp), stacked(bs * ss)
    st_p, st_s, v_s = [], [], []
    for l in range(depth):
        lw = _LayerWeights(wts, l)
        xp, kv_p, (s_p, _) = _layer(xp, lw, rope_p, alpha, bp, sp, None, kv_p, emb if l == 0 else None, tm_p)
        xs, kv_s, (s_s, v) = _layer(xs, lw, rope_s, alpha, bs, ss, past, kv_s, emb if l == 0 else None, tm_s)
        st_p.append(s_p)
        st_s.append(s_s)
        v_s.append(v.reshape(bs, ss, -1))
    return (xp.reshape(bp, sp, d), xs.reshape(bs, ss, d),
            kv_p[0].reshape(depth, bp, sp, -1), kv_p[1].reshape(depth, bp, sp, -1), jnp.stack(st_p),
            kv_s[0].reshape(depth, bs, ss, -1), kv_s[1].reshape(depth, bs, ss, -1), jnp.stack(st_s),
            jnp.stack(v_s))
```

```python
import functools

import numpy as np
import jax
import jax.numpy as jnp
from jax import lax
from jax.experimental import pallas as pl
from jax.experimental.pallas import tpu as pltpu

F32 = jnp.float32
BF16 = jnp.bfloat16

CHUNK = 64
MLA_HEADS = 8
MLA_D_NOPE = 64
MLA_D_ROPE = 32
MLA_D_V = 64
MLA_RANK = 256
MLA_SCALE = (MLA_D_NOPE + MLA_D_ROPE) ** -0.5
LOG2E = float(np.log2(np.e))
Q_SCALE = MLA_SCALE * LOG2E
ROPE_THETA = 10000.0
MASK_VALUE = -1e30
HG_HEADS = 4
HG_DK = 128
HG_DV = 128
HG_W = HG_HEADS * HG_DK
LB_TINY = 1e-30
LOGF_SPLIT = 3
CM_CHUNK = 128
CM_GROUPS = 4
CM_WIDTH = 512
CM_GROUP_DIM = CM_WIDTH // CM_GROUPS
EPS = 1e-5

LANES = 128
VMEM_LIMIT_BYTES = 56 * 1024 * 1024
QK_WIDTH = MLA_RANK + LANES
KV_BLOCK = 256
Q_BLOCK = 256
ROW_TILE = 512
WIDE_ROW_TILE = 512

_A_CQ, _A_CKV, _A_END = 0, 256, 512
_B_HQ, _B_HF, _B_HI, _B_HG, _B_CU, _B_CV, _B_GA, _B_GB, _B_GC, _B_END = (
    0, 512, 1024, 1536, 2048, 2560, 3072, 4096, 5120, 6144)


def _const_spec(shape):
    nd = len(shape)
    return pl.BlockSpec(shape, lambda *_: (0,) * nd, pipeline_mode=pl.Buffered(1))


class _LayerWeights:
    def __init__(self, stacked, layer):
        self.stacked = stacked
        self.layer = layer

    def __getitem__(self, name):
        return self.stacked[name]

    def spec(self, name):
        shape = self.stacked[name].shape[1:]
        layer, nd = self.layer, len(shape)
        return pl.BlockSpec((None,) + shape, lambda *_: (layer,) + (0,) * nd, pipeline_mode=pl.Buffered(1))


def _params(*sem):
    return pltpu.CompilerParams(dimension_semantics=sem, vmem_limit_bytes=VMEM_LIMIT_BYTES)


def _layer_norm(x, g, b):
    mu = jnp.mean(x, -1, keepdims=True)
    xc = x - mu
    var = jnp.mean(xc * xc, -1, keepdims=True)
    return xc * lax.rsqrt(var + EPS) * g + b


def _rms_norm(x, g):
    return x * lax.rsqrt(jnp.mean(x * x, -1, keepdims=True) + EPS) * g


def _gelu(x):
    return 0.5 * x * (1.0 + lax.erf(x * np.float32(1.0 / np.sqrt(2.0))))


def _sigmoid(x):
    return 1.0 / (1.0 + jnp.exp(-x))


def _silu(x):
    return x * _sigmoid(x)


def _log_sigmoid(x):
    return jnp.minimum(x, 0.0) - jnp.log(1.0 + jnp.exp(-jnp.abs(x)))


def _proj_kernel(n_alias, pre_ln, x_ref, wa_ref, wkr_ref, wb_ref, wuq_ref, wuk_ref, qg_ref, kvg_ref, rope_ref,
                 lbp_ref, cmg_ref, cmb_ref, *rest):
    (q_ref, kv_ref, ckv_ref, kr_ref, hq_ref, hk_ref, hlf_ref, hv_ref, hgt_ref,
     u_ref, vcm_ref, sga_ref, sgb_ref, sgc_ref) = rest[n_alias + 2 * pre_ln:][:14]
    x = x_ref[...]
    if pre_ln:
        x = _layer_norm(x, rest[0][...], rest[1][...])
        rest[-1][...] = x
    xb = x.astype(BF16)

    def proj_a(lo, hi):
        return jnp.dot(xb, wa_ref[:, lo:hi], preferred_element_type=F32)

    def proj(lo, hi):
        return jnp.dot(xb, wb_ref[:, lo:hi], preferred_element_type=F32)

    cos_t = rope_ref[:, :LANES]
    sin_up = rope_ref[:, LANES:2 * LANES]
    sin_dn = rope_ref[:, 2 * LANES:]
    half = MLA_D_ROPE // 2

    def rotary(t):
        return (t * cos_t + pltpu.roll(t, half, axis=1) * sin_up
                + pltpu.roll(t, LANES - half, axis=1) * sin_dn)

    cqn = _rms_norm(proj_a(_A_CQ, _A_CKV), qg_ref[...]).astype(BF16)
    q3 = jnp.dot(cqn, wuq_ref[...], preferred_element_type=F32)
    hw = MLA_HEADS * MLA_D_NOPE
    for i in range(MLA_HEADS // 2):
        nope = q3[:, i * LANES:(i + 1) * LANES].astype(BF16)
        lat = jnp.dot(nope, wuk_ref[i], preferred_element_type=F32) * Q_SCALE
        q_ref[2 * i, :, :MLA_RANK] = lat[:, :MLA_RANK].astype(BF16)
        q_ref[2 * i + 1, :, :MLA_RANK] = lat[:, MLA_RANK:].astype(BF16)
    for h in range(MLA_HEADS):
        rot = rotary(q3[:, hw + h * LANES:hw + (h + 1) * LANES]) * Q_SCALE
        q_ref[h, :, MLA_RANK:] = rot.astype(BF16)

    ckv = _rms_norm(proj_a(_A_CKV, _A_END), kvg_ref[...])
    ckv_ref[...] = ckv
    kv_ref[:, :MLA_RANK] = ckv.astype(BF16)
    zk = jnp.dot(xb, wkr_ref[...], preferred_element_type=F32)
    krot = rotary(zk)
    kr_ref[...] = krot[:, :MLA_D_ROPE]
    kv_ref[:, MLA_RANK:] = krot.astype(BF16)

    hq_ref[...] = proj(_B_HQ, _B_HF)
    zf = proj(_B_HF, _B_HI)
    log_lb = lbp_ref[0:1, :]
    log_1m = lbp_ref[1:2, :]
    one_m = lbp_ref[2:3, :]
    c = log_1m + _log_sigmoid(zf)
    hlf_ref[...] = jnp.maximum(log_lb, c) + jnp.log(1.0 + jnp.exp(-jnp.abs(log_lb - c)))
    hk_ref[...] = one_m * _sigmoid(-zf)
    hv_ref[...] = proj(_B_HI, _B_HG).astype(BF16)
    hgt_ref[...] = _silu(proj(_B_HG, _B_CU)).astype(BF16)

    u_ref[...] = _gelu(proj(_B_CU, _B_CV)).astype(BF16)
    vcm_ref[...] = _layer_norm(_gelu(proj(_B_CV, _B_GA)), cmg_ref[...], cmb_ref[...]).astype(vcm_ref.dtype)

    sga_ref[...] = _sigmoid(proj(_B_GA, _B_GB)).astype(BF16)
    sgb_ref[...] = _sigmoid(proj(_B_GB, _B_GC)).astype(BF16)
    sgc_ref[...] = _sigmoid(proj(_B_GC, _B_END)).astype(BF16)


def _proj(x, lw, rope_tab, tm, vcm_dtype, prev, emb):
    m, d = x.shape
    depth = lw["w_b"].shape[0]
    layer = lw.layer
    row = lambda w: pl.BlockSpec((tm, w), lambda i: (i, 0))
    lrow = lambda w: pl.BlockSpec((None, tm, w), lambda i: (layer, i, 0))
    out_shape = [
        jax.ShapeDtypeStruct((MLA_HEADS, m, QK_WIDTH), BF16),
        jax.ShapeDtypeStruct((m, QK_WIDTH), BF16),
        jax.ShapeDtypeStruct((depth, m, MLA_RANK), F32),
        jax.ShapeDtypeStruct((depth, m, MLA_D_ROPE), F32),
        jax.ShapeDtypeStruct((m, HG_W), F32),
        jax.ShapeDtypeStruct((m, HG_W), F32),
        jax.ShapeDtypeStruct((m, HG_W), F32),
        jax.ShapeDtypeStruct((m, HG_W), BF16),
        jax.ShapeDtypeStruct((m, HG_W), BF16),
        jax.ShapeDtypeStruct((m, CM_WIDTH), BF16),
        jax.ShapeDtypeStruct((m, CM_WIDTH), vcm_dtype),
        jax.ShapeDtypeStruct((m, d), BF16),
        jax.ShapeDtypeStruct((m, d), BF16),
        jax.ShapeDtypeStruct((m, d), BF16),
    ]
    out_specs = [
        pl.BlockSpec((MLA_HEADS, tm, QK_WIDTH), lambda i: (0, i, 0)),
        row(QK_WIDTH), lrow(MLA_RANK), lrow(MLA_D_ROPE),
        row(HG_W), row(HG_W), row(HG_W), row(HG_W), row(HG_W),
        row(CM_WIDTH), row(CM_WIDTH), row(d), row(d), row(d),
    ]
    names = ("w_a", "w_kr", "w_b", "w_uq", "w_uk", "q_norm_g", "kv_norm_g")
    tail = ("lbp", "cm_ln_g", "cm_ln_b")
    in_specs = ([row(d)] + [lw.spec(n) for n in names]
                + [pl.BlockSpec((tm, 3 * LANES), lambda i: (i % (rope_tab.shape[0] // tm), 0))]
                + [lw.spec(n) for n in tail])
    args = [x] + [lw[n] for n in names] + [rope_tab] + [lw[n] for n in tail]
    if emb is not None:
        in_specs += [_const_spec((1, d))] * 2
        args += list(emb)
        out_shape.append(jax.ShapeDtypeStruct((m, d), F32))
        out_specs.append(row(d))
    aliases = {len(args): 2, len(args) + 1: 3}
    in_specs += [pl.BlockSpec(memory_space=pl.ANY)] * 2
    args += list(prev)
    return pl.pallas_call(
        functools.partial(_proj_kernel, len(aliases), emb is not None), grid=(m // tm,), in_specs=in_specs,
        out_specs=out_specs, out_shape=out_shape, input_output_aliases=aliases,
        compiler_params=_params("parallel"), name="proj",
    )(*args)


def _uv_project(o, wuv_ref, o_ref, tq):
    ob = o.astype(BF16)
    for i in range(MLA_HEADS // 2):
        pair = (jnp.dot(ob[2 * i * tq:(2 * i + 1) * tq], wuv_ref[2 * i], preferred_element_type=F32)
                + jnp.dot(ob[(2 * i + 1) * tq:(2 * i + 2) * tq], wuv_ref[2 * i + 1], preferred_element_type=F32))
        o_ref[:, i * LANES:(i + 1) * LANES] = pair.astype(BF16)


def _attn_prompt_kernel(q_ref, k_ref, wuv_ref, o_ref, m_ref, l_ref, acc_ref):
    qb = pl.program_id(1)
    tq = q_ref.shape[1]
    rows = MLA_HEADS * tq
    q = q_ref[...].reshape(rows, QK_WIDTH)
    kv_len = (qb + 1) * tq
    nblk = (kv_len + KV_BLOCK - 1) // KV_BLOCK
    rep = KV_BLOCK // LANES

    def scores(j):
        start = pl.multiple_of(j * KV_BLOCK, KV_BLOCK)
        return lax.dot_general(q, k_ref[pl.ds(start, KV_BLOCK), :], (((1,), (1,)), ((), ())),
                               preferred_element_type=F32)

    def values(j):
        start = pl.multiple_of(j * KV_BLOCK, KV_BLOCK)
        return k_ref[pl.ds(start, KV_BLOCK), :MLA_RANK]

    def lane_sums(p):
        return sum(p[:, i * LANES:(i + 1) * LANES] for i in range(rep))

    def update(j, s):
        m_prev = m_ref[...]
        m_new = jnp.maximum(m_prev, jnp.max(s, -1, keepdims=True))
        alpha = jnp.exp2(m_prev - m_new)
        p = jnp.exp2(s - jnp.tile(m_new, (1, rep)))
        l_ref[...] = alpha * l_ref[...] + lane_sums(p)
        acc_ref[...] = jnp.tile(alpha, (1, MLA_RANK // LANES)) * acc_ref[...] + jnp.dot(
            p.astype(BF16), values(j), preferred_element_type=F32)
        m_ref[...] = m_new

    s_diag = scores(nblk - 1)
    s_first = scores(0)
    lane = lax.broadcasted_iota(jnp.int32, (CHUNK, KV_BLOCK), 1)
    visible = [lane < qb * tq + (c + 1) * CHUNK - (nblk - 1) * KV_BLOCK for c in range(tq // CHUNK)]
    s_diag = jnp.concatenate(
        [jnp.where(visible[c], s_diag[h * tq + c * CHUNK:h * tq + (c + 1) * CHUNK], MASK_VALUE)
         for h in range(MLA_HEADS) for c in range(tq // CHUNK)], 0)
    m_diag = jnp.broadcast_to(jnp.max(s_diag, -1, keepdims=True), m_ref.shape)
    p_diag = jnp.exp2(s_diag - jnp.tile(m_diag, (1, rep)))
    m_ref[...] = m_diag
    l_ref[...] = lane_sums(p_diag)
    acc_ref[...] = jnp.dot(p_diag.astype(BF16), values(nblk - 1), preferred_element_type=F32)

    def body(j, s):
        s_next = scores(j + 1)
        update(j, s)
        return s_next

    s_last = lax.fori_loop(0, nblk - 2, body, s_first)

    @pl.when(nblk >= 2)
    def _():
        update(nblk - 2, s_last)

    l_tot = jnp.broadcast_to(jnp.sum(l_ref[...], -1, keepdims=True), l_ref.shape)
    o = acc_ref[...] * jnp.tile(1.0 / l_tot, (1, MLA_RANK // LANES))
    _uv_project(o, wuv_ref, o_ref, tq)


def _attn_prompt(q, k, lw, batch, seq, tq):
    m = batch * seq
    nqc = seq // tq
    rows = MLA_HEADS * tq
    assert tq & (tq - 1) == 0 and tq % CHUNK == 0 and seq % tq == 0
    return pl.pallas_call(
        _attn_prompt_kernel, grid=(batch, nqc),
        in_specs=[pl.BlockSpec((MLA_HEADS, tq, QK_WIDTH), lambda b, c: (0, b * nqc + c, 0)),
                  pl.BlockSpec((seq, QK_WIDTH), lambda b, c: (b, 0)), lw.spec("w_uvp")],
        out_specs=pl.BlockSpec((tq, MLA_HEADS * MLA_D_V), lambda b, c: (b * nqc + c, 0)),
        out_shape=jax.ShapeDtypeStruct((m, MLA_HEADS * MLA_D_V), BF16),
        scratch_shapes=[pltpu.VMEM((rows, LANES), F32), pltpu.VMEM((rows, LANES), F32),
                        pltpu.VMEM((rows, MLA_RANK), F32)],
        compiler_params=_params("parallel", "arbitrary"), name="attn_prompt",
    )(q, k, lw["w_uvp"])


def _attn_sample_kernel(past_len, q_ref, k_ref, cckv_ref, ckrt_ref, wuv_ref, o_ref):
    nq = q_ref.shape[1]
    rows = MLA_HEADS * nq
    q = q_ref[...].reshape(rows, QK_WIDTH)
    q_lat = q[:, :MLA_RANK]
    q_rot = q[:, MLA_RANK:MLA_RANK + MLA_D_ROPE]
    nt = (((1,), (1,)), ((), ()))
    kc = cckv_ref[...].astype(BF16)
    s1 = (lax.dot_general(q_lat, kc, nt, preferred_element_type=F32)
          + jnp.dot(q_rot, ckrt_ref[...].astype(BF16), preferred_element_type=F32))
    kn = k_ref[...]
    s2 = lax.dot_general(q, kn, nt, preferred_element_type=F32)
    q_pos = past_len + lax.broadcasted_iota(jnp.int32, s2.shape, 0) % nq
    k_pos = past_len + lax.broadcasted_iota(jnp.int32, s2.shape, 1)
    s2 = jnp.where(k_pos // CHUNK <= q_pos // CHUNK, s2, MASK_VALUE)
    mx = jnp.maximum(jnp.max(s1, -1, keepdims=True), jnp.max(s2, -1, keepdims=True))
    p1 = jnp.exp2(s1 - mx)
    p2 = jnp.exp2(s2 - mx)
    l = jnp.sum(p1, -1, keepdims=True) + jnp.sum(p2, -1, keepdims=True)
    o = (jnp.dot(p1.astype(BF16), kc, preferred_element_type=F32)
         + jnp.dot(p2.astype(BF16), kn[:, :MLA_RANK], preferred_element_type=F32)) * (1.0 / l)
    _uv_project(o, wuv_ref, o_ref, nq)


def _attn_sample(q, k, cache_ckv, cache_krt, lw, batch, nq):
    past_len = cache_ckv.shape[2]
    layer = lw.layer
    return pl.pallas_call(
        functools.partial(_attn_sample_kernel, past_len), grid=(batch,),
        in_specs=[pl.BlockSpec((MLA_HEADS, nq, QK_WIDTH), lambda b: (0, b, 0)),
                  pl.BlockSpec((nq, QK_WIDTH), lambda b: (b, 0)),
                  pl.BlockSpec((None, None, past_len, MLA_RANK), lambda b: (layer, b, 0, 0)),
                  pl.BlockSpec((None, None, MLA_D_ROPE, past_len), lambda b: (layer, b, 0, 0)),
                  lw.spec("w_uvp")],
        out_specs=pl.BlockSpec((nq, MLA_HEADS * MLA_D_V), lambda b: (b, 0)),
        out_shape=jax.ShapeDtypeStruct((batch * nq, MLA_HEADS * MLA_D_V), BF16),
        compiler_params=_params("parallel"), name="attn_sample",
    )(q, k, cache_ckv, cache_krt, lw["w_uvp"])


def _hgrn_levels(lc):
    spans = []
    sp = lc // 2
    while sp >= 1:
        spans.append(sp)
        sp //= 2
    return spans


def _hgrn_consts(lc):
    spans = _hgrn_levels(lc)
    tri = np.tril(np.ones((lc, lc), np.float32))
    t = np.arange(lc)
    mats, masks = [tri], []
    for sp in spans:
        blk = t // (2 * sp)
        ref_row = blk * 2 * sp + sp - 1
        mats.append(np.abs(tri - tri[ref_row]))
        right = (t % (2 * sp)) >= sp
        masks.append(((blk[:, None] == blk[None, :]) & right[:, None] & ~right[None, :]).astype(np.float32))
    return np.tile(np.concatenate(mats, 0), (1, LOGF_SPLIT)), np.stack(masks, 0)


def _hgrn_kernel(has_init, q_ref, k_ref, lf_ref, v_ref, gt_ref, gn_ref, cmat_ref, mask_ref, *rest):
    s0_ref = rest[0] if has_init else None
    o_ref, sout_ref, st_ref = rest[-3:]
    c = pl.program_id(1)
    nb, lc, _ = q_ref.shape
    nlev = mask_ref.shape[0]
    nt = (((1,), (1,)), ((), ()))
    tn = (((0,), (0,)), ((), ()))

    @pl.when(c == 0)
    def _():
        for bi in range(nb):
            for h in range(HG_HEADS):
                if has_init:
                    st_ref[bi, h] = s0_ref[bi, h].T
                else:
                    st_ref[bi, h] = jnp.zeros((HG_DV, HG_DK), F32)

    eye = (lax.broadcasted_iota(jnp.int32, (lc, lc), 0) == lax.broadcasted_iota(jnp.int32, (lc, lc), 1))
    lvl_mask = [mask_ref[lv] > 0.5 for lv in range(nlev)]
    for bi in range(nb):
        lf = lf_ref[bi] * LOG2E
        parts, rem = [], lf
        for _ in range(LOGF_SPLIT):
            parts.append(rem.astype(BF16))
            rem = rem - parts[-1].astype(F32)
        br = jnp.dot(cmat_ref[...], jnp.concatenate(parts, 0), preferred_element_type=F32)
        for h in range(HG_HEADS):
            sl = slice(h * HG_DK, (h + 1) * HG_DK)
            bh = br[:lc, sl]
            qh = q_ref[bi, :, sl]
            kh = k_ref[bi, :, sl]
            vh = v_ref[bi, :, sl]
            a = jnp.where(eye, jnp.sum(qh * kh, -1, keepdims=True), 0.0)
            for lv in range(nlev):
                e = jnp.exp2(br[(1 + lv) * lc:(2 + lv) * lc, sl])
                a = jnp.where(lvl_mask[lv],
                              lax.dot_general((qh * e).astype(BF16), (kh * e).astype(BF16), nt,
                                              preferred_element_type=F32), a)
            st = st_ref[bi, h]
            o = (jnp.dot(a.astype(BF16), vh, preferred_element_type=F32)
                 + lax.dot_general((qh * jnp.exp2(bh)).astype(BF16), st.astype(BF16), nt,
                                   preferred_element_type=F32))
            b_last = bh[lc - 1:lc, :]
            kdec = (kh * jnp.exp2(b_last - bh)).astype(BF16)
            st_ref[bi, h] = st * jnp.exp2(b_last) + lax.dot_general(vh, kdec, tn, preferred_element_type=F32)
            on = _rms_norm(o, gn_ref[:, sl])
            o_ref[bi, :, sl] = (on * gt_ref[bi, :, sl].astype(F32)).astype(BF16)

    @pl.when(c == pl.num_programs(1) - 1)
    def _():
        for bi in range(nb):
            for h in range(HG_HEADS):
                sout_ref[bi, h] = st_ref[bi, h].T


def _hgrn(hq, hk, hlf, hv, hgt, lw, state, batch, seq, lc, nb):
    nch = seq // lc
    assert batch % nb == 0
    cmat, masks = _hgrn_consts(lc)
    has_init = state is not None
    layer = lw.layer
    row = pl.BlockSpec((nb, lc, HG_W), lambda b, c: (b, c, 0))
    st_spec = pl.BlockSpec((nb, HG_HEADS, HG_DK, HG_DV), lambda b, c: (b, 0, 0, 0))
    in_specs = [row, row, row, row, row, lw.spec("hg_norm_g"), _const_spec(cmat.shape), _const_spec(masks.shape)]
    args = [t.reshape(batch, seq, HG_W) for t in (hq, hk, hlf, hv, hgt)]
    args += [lw["hg_norm_g"], jnp.asarray(cmat, BF16), jnp.asarray(masks)]
    if has_init:
        in_specs.append(pl.BlockSpec((None, nb, HG_HEADS, HG_DK, HG_DV), lambda b, c: (layer, b, 0, 0, 0)))
        args.append(state)
    o, s_fin = pl.pallas_call(
        functools.partial(_hgrn_kernel, has_init), grid=(batch // nb, nch),
        in_specs=in_specs, out_specs=[row, st_spec],
        out_shape=[jax.ShapeDtypeStruct((batch, seq, HG_W), BF16),
                   jax.ShapeDtypeStruct((batch, HG_HEADS, HG_DK, HG_DV), F32)],
        scratch_shapes=[pltpu.VMEM((nb, HG_HEADS, HG_DV, HG_DK), F32)],
        compiler_params=_params("parallel", "arbitrary"), name="hgrn",
    )(*args)
    return o.reshape(batch * seq, HG_W), s_fin


def _merge_kernel(alpha, cml, x_ref, ain_ref, hg_ref, u_ref, vcm_ref, sga_ref, sgb_ref, sgc_ref,
                  wpa_ref, wpb_ref, wpc_ref, wo_ref, ws_ref, bs_ref, g_ref, b_ref,
                  o_ref, cin_ref):
    tm = x_ref.shape[0]
    tril = (lax.broadcasted_iota(jnp.int32, (cml, cml), 0) >= lax.broadcasted_iota(jnp.int32, (cml, cml), 1))
    for g in range(CM_GROUPS):
        sl = slice(g * CM_GROUP_DIM, (g + 1) * CM_GROUP_DIM)
        wg = jnp.where(tril, ws_ref[g, :cml, :cml], 0.0).astype(BF16)
        for r in range(tm // cml):
            rs = slice(r * cml, (r + 1) * cml)
            s = jnp.dot(wg, vcm_ref[rs, sl].astype(BF16), preferred_element_type=F32) + bs_ref[:cml, sl]
            cin_ref[rs, sl] = (u_ref[rs, sl].astype(F32) * s).astype(BF16)
    y_a = jnp.dot(ain_ref[...], wpa_ref[...], preferred_element_type=F32)
    y_b = jnp.dot(hg_ref[...], wpb_ref[...], preferred_element_type=F32)
    y_c = jnp.dot(cin_ref[...], wpc_ref[...], preferred_element_type=F32)
    mrg = (sga_ref[...].astype(F32) * y_a + sgb_ref[...].astype(F32) * y_b
           + sgc_ref[...].astype(F32) * y_c).astype(BF16)
    y = alpha * x_ref[...] + jnp.dot(mrg, wo_ref[...], preferred_element_type=F32)
    o_ref[...] = _layer_norm(y, g_ref[...], b_ref[...])


def _merge(x, a_in, hg_o, u, vcm, sga, sgb, sgc, lw, alpha, cml, tm):
    m, d = x.shape
    row = lambda w: pl.BlockSpec((tm, w), lambda i: (i, 0))
    in_specs = [row(d), row(MLA_HEADS * MLA_D_V), row(HG_W), row(CM_WIDTH), row(CM_WIDTH), row(d), row(d), row(d),
                lw.spec("w_pa"), lw.spec("w_pb"), lw.spec("w_pc"), lw.spec("w_o"),
                lw.spec("cm_ws"), lw.spec("cm_bs_full"), lw.spec("ln1_g"), lw.spec("ln1_b")]
    return pl.pallas_call(
        functools.partial(_merge_kernel, alpha, cml), grid=(m // tm,), in_specs=in_specs,
        out_specs=row(d), out_shape=jax.ShapeDtypeStruct((m, d), F32),
        scratch_shapes=[pltpu.VMEM((tm, CM_WIDTH), BF16)],
        compiler_params=_params("parallel"), name="merge",
    )(x, a_in, hg_o, u, vcm, sga, sgb, sgc, lw["w_pa"], lw["w_pb"], lw["w_pc"], lw["w_o"],
      lw["cm_ws"], lw["cm_bs_full"], lw["ln1_g"], lw["ln1_b"])


def _ffn_kernel(alpha, nsplit, x_ref, wup_ref, wdn_ref, g_ref, b_ref, o_ref):
    x = x_ref[...]
    xb = x.astype(BF16)
    dff = wdn_ref.shape[0]
    cw = dff // nsplit
    y = alpha * x
    for j in range(nsplit):
        gate = jnp.dot(xb, wup_ref[:, j * cw:(j + 1) * cw], preferred_element_type=F32)
        up = jnp.dot(xb, wup_ref[:, dff + j * cw:dff + (j + 1) * cw], preferred_element_type=F32)
        act = (_silu(gate) * up).astype(BF16)
        y = y + jnp.dot(act, wdn_ref[j * cw:(j + 1) * cw, :], preferred_element_type=F32)
    o_ref[...] = _layer_norm(y, g_ref[...], b_ref[...])


def _ffn(x, lw, alpha, tm):
    m, d = x.shape
    row = pl.BlockSpec((tm, d), lambda i: (i, 0))
    return pl.pallas_call(
        functools.partial(_ffn_kernel, alpha, 1), grid=(m // tm,),
        in_specs=[row, lw.spec("w_up"), lw.spec("w_down"), lw.spec("ln2_g"), lw.spec("ln2_b")],
        out_specs=row, out_shape=jax.ShapeDtypeStruct((m, d), F32),
        compiler_params=_params("parallel"), name="ffn",
    )(x, lw["w_up"], lw["w_down"], lw["ln2_g"], lw["ln2_b"])


def _rope_table(pos):
    half = MLA_D_ROPE // 2
    inv = 1.0 / (ROPE_THETA ** (jnp.arange(half, dtype=F32) / half))
    ang = pos.astype(F32)[:, None] * inv[None]
    cos, sin = jnp.cos(ang), jnp.sin(ang)
    z = jnp.zeros((pos.shape[0], LANES - MLA_D_ROPE), F32)
    zh = jnp.zeros_like(sin)
    return jnp.concatenate([cos, cos, z, zh, sin, z, -sin, zh, z], -1)


def _prep_weights(w_in, q_norm_g, w_uq, w_uk, kv_norm_g, w_uv, hg_lb, hg_norm_g, cm_ln_g, cm_ln_b,
                  cm_ws, cm_bs, w_pa, w_pb, w_pc, w_o, ln1_g, ln1_b, w_up, w_down, ln2_g, ln2_b):
    depth, d, _ = w_in.shape
    o_kr = 2 * MLA_RANK
    assert w_in.shape[2] - o_kr - MLA_D_ROPE == _B_END
    w_a = w_in[:, :, :o_kr].astype(BF16)
    w_b = w_in[:, :, o_kr + MLA_D_ROPE:].astype(BF16)
    kr = w_in[:, :, o_kr:o_kr + MLA_D_ROPE]
    zk = jnp.zeros((depth, d, LANES - MLA_D_ROPE), F32)
    w_kr = jnp.concatenate([kr, zk], -1).astype(BF16)

    dq = MLA_D_NOPE + MLA_D_ROPE
    uq = w_uq.reshape(depth, MLA_RANK, MLA_HEADS, dq)
    nope, rot = uq[..., :MLA_D_NOPE], uq[..., MLA_D_NOPE:]
    zr = jnp.zeros((depth, MLA_RANK, MLA_HEADS, LANES - MLA_D_ROPE), F32)
    w_uq_r = jnp.concatenate([nope.reshape(depth, MLA_RANK, MLA_HEADS * MLA_D_NOPE),
                              jnp.concatenate([rot, zr], -1).reshape(depth, MLA_RANK, MLA_HEADS * LANES)],
                             -1).astype(BF16)
    ukt = jnp.transpose(w_uk, (0, 2, 3, 1)).reshape(depth, MLA_HEADS // 2, 2, MLA_D_NOPE, MLA_RANK)
    eye2 = jnp.eye(2, dtype=F32)
    w_uk_r = (ukt[:, :, :, :, None, :] * eye2[None, None, :, None, :, None]).reshape(
        depth, MLA_HEADS // 2, 2 * MLA_D_NOPE, 2 * MLA_RANK).astype(BF16)
    uvt = jnp.transpose(w_uv, (0, 2, 1, 3)).astype(BF16)
    lane_half = (jnp.arange(2 * MLA_D_V) // MLA_D_V)[None, None, None, :]
    head_half = (jnp.arange(MLA_HEADS) % 2)[None, :, None, None]
    w_uvp = jnp.where(lane_half == head_half, jnp.concatenate([uvt, uvt], -1), jnp.zeros((), BF16))

    sm = jax.nn.softmax(hg_lb.astype(F32), axis=0)
    lb = jnp.concatenate([jnp.zeros_like(sm[:1]), jnp.cumsum(sm[1:], axis=0)], axis=0)
    lbp = jnp.stack([jnp.log(lb + LB_TINY), jnp.log1p(-lb), 1.0 - lb], 1)

    bs_full = jnp.repeat(jnp.transpose(cm_bs, (0, 2, 1)), CM_GROUP_DIM, axis=-1)

    r3 = lambda t: t.reshape(depth, 1, -1)
    return dict(w_a=w_a, w_kr=w_kr, w_b=w_b, w_uq=w_uq_r, w_uk=w_uk_r, w_uvp=w_uvp, lbp=lbp,
                q_norm_g=r3(q_norm_g), kv_norm_g=r3(kv_norm_g), hg_norm_g=r3(hg_norm_g),
                cm_ln_g=r3(cm_ln_g), cm_ln_b=r3(cm_ln_b), cm_ws=cm_ws, cm_bs_full=bs_full,
                w_pa=w_pa.astype(BF16), w_pb=w_pb.astype(BF16), w_pc=w_pc.astype(BF16), w_o=w_o.astype(BF16),
                ln1_g=r3(ln1_g), ln1_b=r3(ln1_b), w_up=w_up.astype(BF16), w_down=w_down.astype(BF16),
                ln2_g=r3(ln2_g), ln2_b=r3(ln2_b))


def _hgrn_rows(batch):
    for nb in (8, 4, 2, 1):
        if batch % nb == 0:
            return nb


def _row_tile(m, want):
    tm = min(m, want)
    assert m % tm == 0, (m, tm)
    return tm


def _layer(x, lw, rope_tab, alpha, batch, seq, past, prev, emb, tm):
    prompt = past is None
    outs = _proj(x, lw, rope_tab, tm, BF16 if prompt else F32, prev, emb)
    if emb is not None:
        x = outs[-1]
    q, k, ckv, kr, hq, hk, hlf, hv, hgt, u, vcm, sga, sgb, sgc = outs[:14]
    if prompt:
        a_in = _attn_prompt(q, k, lw, batch, seq, Q_BLOCK)
        hg_o, s_fin = _hgrn(hq, hk, hlf, hv, hgt, lw, None, batch, seq, CHUNK, _hgrn_rows(batch))
    else:
        a_in = _attn_sample(q, k, past[0], past[1], lw, batch, seq)
        hg_o, s_fin = _hgrn(hq, hk, hlf, hv, hgt, lw, past[2], batch, seq, seq, _hgrn_rows(batch))
    cml = min(seq, CM_CHUNK)
    tm_wide = _row_tile(x.shape[0], WIDE_ROW_TILE)
    x = _merge(x, a_in, hg_o, u, vcm, sga, sgb, sgc, lw, alpha, cml, tm_wide)
    x = _ffn(x, lw, alpha, tm_wide)
    return x, (ckv, kr), (s_fin, vcm)


def kernel(x_prompt, x_sample, cache_mla_ckv, cache_mla_krope, state_hgrn, emb_ln_g, emb_ln_b, w_in, q_norm_g, w_uq, w_uk, kv_norm_g, w_uv, hg_lb, hg_norm_g, cm_ln_g, cm_ln_b, cm_ws, cm_bs, w_pa, w_pb, w_pc, w_o, ln1_g, ln1_b, w_up, w_down, ln2_g, ln2_b):
    bp, sp, d = x_prompt.shape
    bs, ss, _ = x_sample.shape
    depth = w_in.shape[0]
    past_len = cache_mla_ckv.shape[2]
    assert sp % CM_CHUNK == 0 and sp % KV_BLOCK == 0 and ss <= CHUNK and ss % 16 == 0
    alpha = float((2 * depth) ** 0.25)

    wts = _prep_weights(w_in, q_norm_g, w_uq, w_uk, kv_norm_g, w_uv, hg_lb, hg_norm_g, cm_ln_g, cm_ln_b,
                        cm_ws, cm_bs, w_pa, w_pb, w_pc, w_o, ln1_g, ln1_b, w_up, w_down, ln2_g, ln2_b)
    tm_p = _row_tile(bp * sp, ROW_TILE)
    tm_s = _row_tile(bs * ss, ROW_TILE)
    assert (sp % tm_p == 0 or tm_p % sp == 0) and (ss % tm_s == 0 or tm_s % ss == 0)
    rope_p = jnp.tile(_rope_table(jnp.arange(sp, dtype=jnp.int32)), (max(1, tm_p // sp), 1))
    rope_s = jnp.tile(_rope_table(past_len + jnp.arange(ss, dtype=jnp.int32)), (max(1, tm_s // ss), 1))
    emb = (emb_ln_g.reshape(1, d), emb_ln_b.reshape(1, d))
    xp = x_prompt.reshape(bp * sp, d)
    xs = x_sample.reshape(bs * ss, d)
    past = (cache_mla_ckv, jnp.swapaxes(cache_mla_krope, -1, -2), state_hgrn)

    stacked = lambda m: (jnp.zeros((depth, m, MLA_RANK), F32), jnp.zeros((depth, m, MLA_D_ROPE), F32))
    kv_p, kv_s = stacked(bp * sp), stacked(bs * ss)
    st_p, st_s, v_s = [], [], []
    for l in range(depth):
        lw = _LayerWeights(wts, l)
        xp, kv_p, (s_p, _) = _layer(xp, lw, rope_p, alpha, bp, sp, None, kv_p, emb if l == 0 else None, tm_p)
        xs, kv_s, (s_s, v) = _layer(xs, lw, rope_s, alpha, bs, ss, past, kv_s, emb if l == 0 else None, tm_s)
        st_p.append(s_p)
        st_s.append(s_s)
        v_s.append(v.reshape(bs, ss, -1))
    return (xp.reshape(bp, sp, d), xs.reshape(bs, ss, d),
            kv_p[0].reshape(depth, bp, sp, -1), kv_p[1].reshape(depth, bp, sp, -1), jnp.stack(st_p),
            kv_s[0].reshape(depth, bs, ss, -1), kv_s[1].reshape(depth, bs, ss, -1), jnp.stack(st_s),
            jnp.stack(v_s))
```

```python
import functools

import numpy as np
import jax
import jax.numpy as jnp
from jax import lax
from jax.experimental import pallas as pl
from jax.experimental.pallas import tpu as pltpu

F32 = jnp.float32
BF16 = jnp.bfloat16

CHUNK = 64
MLA_HEADS = 8
MLA_D_NOPE = 64
MLA_D_ROPE = 32
MLA_D_V = 64
MLA_RANK = 256
MLA_SCALE = (MLA_D_NOPE + MLA_D_ROPE) ** -0.5
LOG2E = float(np.log2(np.e))
Q_SCALE = MLA_SCALE * LOG2E
ROPE_THETA = 10000.0
MASK_VALUE = -1e30
HG_HEADS = 4
HG_DK = 128
HG_DV = 128
HG_W = HG_HEADS * HG_DK
LB_TINY = 1e-30
LOGF_SPLIT = 3
CM_CHUNK = 128
CM_GROUPS = 4
CM_WIDTH = 512
CM_GROUP_DIM = CM_WIDTH // CM_GROUPS
EPS = 1e-5

LANES = 128
VMEM_LIMIT_BYTES = 56 * 1024 * 1024
QK_WIDTH = MLA_RANK + LANES
KV_BLOCK = 256
Q_BLOCK = 256
ROW_TILE = 512
WIDE_ROW_TILE = 512
FFN_STREAM_CHUNK = 256

_A_CQ, _A_CKV, _A_END = 0, 256, 512
_B_HQ, _B_HF, _B_HI, _B_HG, _B_CU, _B_CV, _B_GA, _B_GB, _B_GC, _B_END = (
    0, 512, 1024, 1536, 2048, 2560, 3072, 4096, 5120, 6144)


def _const_spec(shape):
    nd = len(shape)
    return pl.BlockSpec(shape, lambda *_: (0,) * nd, pipeline_mode=pl.Buffered(1))


class _LayerWeights:
    def __init__(self, stacked, layer):
        self.stacked = stacked
        self.layer = layer

    def __getitem__(self, name):
        return self.stacked[name]

    def spec(self, name):
        shape = self.stacked[name].shape[1:]
        layer, nd = self.layer, len(shape)
        return pl.BlockSpec((None,) + shape, lambda *_: (layer,) + (0,) * nd, pipeline_mode=pl.Buffered(1))


def _params(*sem):
    return pltpu.CompilerParams(dimension_semantics=sem, vmem_limit_bytes=VMEM_LIMIT_BYTES)


def _layer_norm(x, g, b):
    mu = jnp.mean(x, -1, keepdims=True)
    xc = x - mu
    var = jnp.mean(xc * xc, -1, keepdims=True)
    return xc * lax.rsqrt(var + EPS) * g + b


def _rms_norm(x, g):
    return x * lax.rsqrt(jnp.mean(x * x, -1, keepdims=True) + EPS) * g


def _gelu(x):
    return 0.5 * x * (1.0 + lax.erf(x * np.float32(1.0 / np.sqrt(2.0))))


def _sigmoid(x):
    return 1.0 / (1.0 + jnp.exp(-x))


def _silu(x):
    return x * _sigmoid(x)


def _log_sigmoid(x):
    return jnp.minimum(x, 0.0) - jnp.log(1.0 + jnp.exp(-jnp.abs(x)))


def _proj_kernel(n_alias, pre_ln, x_ref, wa_ref, wkr_ref, wb_ref, wuq_ref, wuk_ref, qg_ref, kvg_ref, rope_ref,
                 lbp_ref, cmg_ref, cmb_ref, *rest):
    (q_ref, kv_ref, ckv_ref, kr_ref, hq_ref, hk_ref, hlf_ref, hv_ref, hgt_ref,
     u_ref, vcm_ref, sga_ref, sgb_ref, sgc_ref) = rest[n_alias + 2 * pre_ln:][:14]
    x = x_ref[...]
    if pre_ln:
        x = _layer_norm(x, rest[0][...], rest[1][...])
        rest[-1][...] = x
    xb = x.astype(BF16)

    def proj_a(lo, hi):
        return jnp.dot(xb, wa_ref[:, lo:hi], preferred_element_type=F32)

    def proj(lo, hi):
        return jnp.dot(xb, wb_ref[:, lo:hi], preferred_element_type=F32)

    cos_t = rope_ref[:, :LANES]
    sin_up = rope_ref[:, LANES:2 * LANES]
    sin_dn = rope_ref[:, 2 * LANES:]
    half = MLA_D_ROPE // 2

    def rotary(t):
        return (t * cos_t + pltpu.roll(t, half, axis=1) * sin_up
                + pltpu.roll(t, LANES - half, axis=1) * sin_dn)

    cqn = _rms_norm(proj_a(_A_CQ, _A_CKV), qg_ref[...]).astype(BF16)
    q3 = jnp.dot(cqn, wuq_ref[...], preferred_element_type=F32)
    hw = MLA_HEADS * MLA_D_NOPE
    for i in range(MLA_HEADS // 2):
        nope = q3[:, i * LANES:(i + 1) * LANES].astype(BF16)
        lat = jnp.dot(nope, wuk_ref[i], preferred_element_type=F32) * Q_SCALE
        q_ref[2 * i, :, :MLA_RANK] = lat[:, :MLA_RANK].astype(BF16)
        q_ref[2 * i + 1, :, :MLA_RANK] = lat[:, MLA_RANK:].astype(BF16)
    for h in range(MLA_HEADS):
        rot = rotary(q3[:, hw + h * LANES:hw + (h + 1) * LANES]) * Q_SCALE
        q_ref[h, :, MLA_RANK:] = rot.astype(BF16)

    ckv = _rms_norm(proj_a(_A_CKV, _A_END), kvg_ref[...])
    ckv_ref[...] = ckv
    kv_ref[:, :MLA_RANK] = ckv.astype(BF16)
    zk = jnp.dot(xb, wkr_ref[...], preferred_element_type=F32)
    krot = rotary(zk)
    kr_ref[...] = krot[:, :MLA_D_ROPE]
    kv_ref[:, MLA_RANK:] = krot.astype(BF16)

    hq_ref[...] = proj(_B_HQ, _B_HF)
    zf = proj(_B_HF, _B_HI)
    log_lb = lbp_ref[0:1, :]
    log_1m = lbp_ref[1:2, :]
    one_m = lbp_ref[2:3, :]
    c = log_1m + _log_sigmoid(zf)
    hlf_ref[...] = jnp.maximum(log_lb, c) + jnp.log(1.0 + jnp.exp(-jnp.abs(log_lb - c)))
    hk_ref[...] = one_m * _sigmoid(-zf)
    hv_ref[...] = proj(_B_HI, _B_HG).astype(BF16)
    hgt_ref[...] = _silu(proj(_B_HG, _B_CU)).astype(BF16)

    u_ref[...] = _gelu(proj(_B_CU, _B_CV)).astype(BF16)
    vcm_ref[...] = _layer_norm(_gelu(proj(_B_CV, _B_GA)), cmg_ref[...], cmb_ref[...]).astype(vcm_ref.dtype)

    sga_ref[...] = _sigmoid(proj(_B_GA, _B_GB)).astype(BF16)
    sgb_ref[...] = _sigmoid(proj(_B_GB, _B_GC)).astype(BF16)
    sgc_ref[...] = _sigmoid(proj(_B_GC, _B_END)).astype(BF16)


def _proj(x, lw, rope_tab, tm, vcm_dtype, prev, emb):
    m, d = x.shape
    depth = lw["w_b"].shape[0]
    layer = lw.layer
    row = lambda w: pl.BlockSpec((tm, w), lambda i: (i, 0))
    lrow = lambda w: pl.BlockSpec((None, tm, w), lambda i: (layer, i, 0))
    out_shape = [
        jax.ShapeDtypeStruct((MLA_HEADS, m, QK_WIDTH), BF16),
        jax.ShapeDtypeStruct((m, QK_WIDTH), BF16),
        jax.ShapeDtypeStruct((depth, m, MLA_RANK), F32),
        jax.ShapeDtypeStruct((depth, m, MLA_D_ROPE), F32),
        jax.ShapeDtypeStruct((m, HG_W), F32),
        jax.ShapeDtypeStruct((m, HG_W), F32),
        jax.ShapeDtypeStruct((m, HG_W), F32),
        jax.ShapeDtypeStruct((m, HG_W), BF16),
        jax.ShapeDtypeStruct((m, HG_W), BF16),
        jax.ShapeDtypeStruct((m, CM_WIDTH), BF16),
        jax.ShapeDtypeStruct((m, CM_WIDTH), vcm_dtype),
        jax.ShapeDtypeStruct((m, d), BF16),
        jax.ShapeDtypeStruct((m, d), BF16),
        jax.ShapeDtypeStruct((m, d), BF16),
    ]
    out_specs = [
        pl.BlockSpec((MLA_HEADS, tm, QK_WIDTH), lambda i: (0, i, 0)),
        row(QK_WIDTH), lrow(MLA_RANK), lrow(MLA_D_ROPE),
        row(HG_W), row(HG_W), row(HG_W), row(HG_W), row(HG_W),
        row(CM_WIDTH), row(CM_WIDTH), row(d), row(d), row(d),
    ]
    names = ("w_a", "w_kr", "w_b", "w_uq", "w_uk", "q_norm_g", "kv_norm_g")
    tail = ("lbp", "cm_ln_g", "cm_ln_b")
    in_specs = ([row(d)] + [lw.spec(n) for n in names]
                + [pl.BlockSpec((tm, 3 * LANES), lambda i: (i % (rope_tab.shape[0] // tm), 0))]
                + [lw.spec(n) for n in tail])
    args = [x] + [lw[n] for n in names] + [rope_tab] + [lw[n] for n in tail]
    if emb is not None:
        in_specs += [_const_spec((1, d))] * 2
        args += list(emb)
        out_shape.append(jax.ShapeDtypeStruct((m, d), F32))
        out_specs.append(row(d))
    aliases = {len(args): 2, len(args) + 1: 3}
    in_specs += [pl.BlockSpec(memory_space=pl.ANY)] * 2
    args += list(prev)
    return pl.pallas_call(
        functools.partial(_proj_kernel, len(aliases), emb is not None), grid=(m // tm,), in_specs=in_specs,
        out_specs=out_specs, out_shape=out_shape, input_output_aliases=aliases,
        compiler_params=_params("parallel"), name="proj",
    )(*args)


def _uv_project(o, wuv_ref, o_ref, tq):
    ob = o.astype(BF16)
    for i in range(MLA_HEADS // 2):
        pair = (jnp.dot(ob[2 * i * tq:(2 * i + 1) * tq], wuv_ref[2 * i], preferred_element_type=F32)
                + jnp.dot(ob[(2 * i + 1) * tq:(2 * i + 2) * tq], wuv_ref[2 * i + 1], preferred_element_type=F32))
        o_ref[:, i * LANES:(i + 1) * LANES] = pair.astype(BF16)


def _attn_prompt_kernel(q_ref, k_ref, wuv_ref, o_ref, m_ref, l_ref, acc_ref):
    qb = pl.program_id(1)
    tq = q_ref.shape[1]
    rows = MLA_HEADS * tq
    q = q_ref[...].reshape(rows, QK_WIDTH)
    kv_len = (qb + 1) * tq
    nblk = (kv_len + KV_BLOCK - 1) // KV_BLOCK
    rep = KV_BLOCK // LANES

    def scores(j):
        start = pl.multiple_of(j * KV_BLOCK, KV_BLOCK)
        return lax.dot_general(q, k_ref[pl.ds(start, KV_BLOCK), :], (((1,), (1,)), ((), ())),
                               preferred_element_type=F32)

    def values(j):
        start = pl.multiple_of(j * KV_BLOCK, KV_BLOCK)
        return k_ref[pl.ds(start, KV_BLOCK), :MLA_RANK]

    def lane_sums(p):
        return sum(p[:, i * LANES:(i + 1) * LANES] for i in range(rep))

    def update(j, s):
        m_prev = m_ref[...]
        m_new = jnp.maximum(m_prev, jnp.max(s, -1, keepdims=True))
        alpha = jnp.exp2(m_prev - m_new)
        p = jnp.exp2(s - jnp.tile(m_new, (1, rep)))
        l_ref[...] = alpha * l_ref[...] + lane_sums(p)
        acc_ref[...] = jnp.tile(alpha, (1, MLA_RANK // LANES)) * acc_ref[...] + jnp.dot(
            p.astype(BF16), values(j), preferred_element_type=F32)
        m_ref[...] = m_new

    s_diag = scores(nblk - 1)
    s_first = scores(0)
    lane = lax.broadcasted_iota(jnp.int32, (CHUNK, KV_BLOCK), 1)
    visible = [lane < qb * tq + (c + 1) * CHUNK - (nblk - 1) * KV_BLOCK for c in range(tq // CHUNK)]
    s_diag = jnp.concatenate(
        [jnp.where(visible[c], s_diag[h * tq + c * CHUNK:h * tq + (c + 1) * CHUNK], MASK_VALUE)
         for h in range(MLA_HEADS) for c in range(tq // CHUNK)], 0)
    m_diag = jnp.broadcast_to(jnp.max(s_diag, -1, keepdims=True), m_ref.shape)
    p_diag = jnp.exp2(s_diag - jnp.tile(m_diag, (1, rep)))
    m_ref[...] = m_diag
    l_ref[...] = lane_sums(p_diag)
    acc_ref[...] = jnp.dot(p_diag.astype(BF16), values(nblk - 1), preferred_element_type=F32)

    def body(j, s):
        s_next = scores(j + 1)
        update(j, s)
        return s_next

    s_last = lax.fori_loop(0, nblk - 2, body, s_first)

    @pl.when(nblk >= 2)
    def _():
        update(nblk - 2, s_last)

    l_tot = jnp.broadcast_to(jnp.sum(l_ref[...], -1, keepdims=True), l_ref.shape)
    o = acc_ref[...] * jnp.tile(1.0 / l_tot, (1, MLA_RANK // LANES))
    _uv_project(o, wuv_ref, o_ref, tq)


def _attn_prompt(q, k, lw, batch, seq, tq):
    m = batch * seq
    nqc = seq // tq
    rows = MLA_HEADS * tq
    assert tq & (tq - 1) == 0 and tq % CHUNK == 0 and seq % tq == 0
    return pl.pallas_call(
        _attn_prompt_kernel, grid=(batch, nqc),
        in_specs=[pl.BlockSpec((MLA_HEADS, tq, QK_WIDTH), lambda b, c: (0, b * nqc + c, 0)),
                  pl.BlockSpec((seq, QK_WIDTH), lambda b, c: (b, 0)), lw.spec("w_uvp")],
        out_specs=pl.BlockSpec((tq, MLA_HEADS * MLA_D_V), lambda b, c: (b * nqc + c, 0)),
        out_shape=jax.ShapeDtypeStruct((m, MLA_HEADS * MLA_D_V), BF16),
        scratch_shapes=[pltpu.VMEM((rows, LANES), F32), pltpu.VMEM((rows, LANES), F32),
                        pltpu.VMEM((rows, MLA_RANK), F32)],
        compiler_params=_params("parallel", "arbitrary"), name="attn_prompt",
    )(q, k, lw["w_uvp"])


def _attn_sample_kernel(past_len, q_ref, k_ref, cckv_ref, ckrt_ref, wuv_ref, o_ref):
    nq = q_ref.shape[1]
    rows = MLA_HEADS * nq
    q = q_ref[...].reshape(rows, QK_WIDTH)
    q_lat = q[:, :MLA_RANK]
    q_rot = q[:, MLA_RANK:MLA_RANK + MLA_D_ROPE]
    nt = (((1,), (1,)), ((), ()))
    kc = cckv_ref[...].astype(BF16)
    s1 = (lax.dot_general(q_lat, kc, nt, preferred_element_type=F32)
          + jnp.dot(q_rot, ckrt_ref[...].astype(BF16), preferred_element_type=F32))
    kn = k_ref[...]
    s2 = lax.dot_general(q, kn, nt, preferred_element_type=F32)
    q_pos = past_len + lax.broadcasted_iota(jnp.int32, s2.shape, 0) % nq
    k_pos = past_len + lax.broadcasted_iota(jnp.int32, s2.shape, 1)
    s2 = jnp.where(k_pos // CHUNK <= q_pos // CHUNK, s2, MASK_VALUE)
    mx = jnp.maximum(jnp.max(s1, -1, keepdims=True), jnp.max(s2, -1, keepdims=True))
    p1 = jnp.exp2(s1 - mx)
    p2 = jnp.exp2(s2 - mx)
    l = jnp.sum(p1, -1, keepdims=True) + jnp.sum(p2, -1, keepdims=True)
    o = (jnp.dot(p1.astype(BF16), kc, preferred_element_type=F32)
         + jnp.dot(p2.astype(BF16), kn[:, :MLA_RANK], preferred_element_type=F32)) * (1.0 / l)
    _uv_project(o, wuv_ref, o_ref, nq)


def _attn_sample(q, k, cache_ckv, cache_krt, lw, batch, nq):
    past_len = cache_ckv.shape[2]
    layer = lw.layer
    return pl.pallas_call(
        functools.partial(_attn_sample_kernel, past_len), grid=(batch,),
        in_specs=[pl.BlockSpec((MLA_HEADS, nq, QK_WIDTH), lambda b: (0, b, 0)),
                  pl.BlockSpec((nq, QK_WIDTH), lambda b: (b, 0)),
                  pl.BlockSpec((None, None, past_len, MLA_RANK), lambda b: (layer, b, 0, 0)),
                  pl.BlockSpec((None, None, MLA_D_ROPE, past_len), lambda b: (layer, b, 0, 0)),
                  lw.spec("w_uvp")],
        out_specs=pl.BlockSpec((nq, MLA_HEADS * MLA_D_V), lambda b: (b, 0)),
        out_shape=jax.ShapeDtypeStruct((batch * nq, MLA_HEADS * MLA_D_V), BF16),
        compiler_params=_params("parallel"), name="attn_sample",
    )(q, k, cache_ckv, cache_krt, lw["w_uvp"])


def _hgrn_levels(lc):
    spans = []
    sp = lc // 2
    while sp >= 1:
        spans.append(sp)
        sp //= 2
    return spans


def _hgrn_consts(lc):
    spans = _hgrn_levels(lc)
    tri = np.tril(np.ones((lc, lc), np.float32))
    t = np.arange(lc)
    mats, masks = [tri], []
    for sp in spans:
        blk = t // (2 * sp)
        ref_row = blk * 2 * sp + sp - 1
        mats.append(np.abs(tri - tri[ref_row]))
        right = (t % (2 * sp)) >= sp
        masks.append(((blk[:, None] == blk[None, :]) & right[:, None] & ~right[None, :]).astype(np.float32))
    return np.tile(np.concatenate(mats, 0), (1, LOGF_SPLIT)), np.stack(masks, 0)


def _hgrn_kernel(has_init, q_ref, k_ref, lf_ref, v_ref, gt_ref, gn_ref, cmat_ref, mask_ref, *rest):
    s0_ref = rest[0] if has_init else None
    o_ref, sout_ref, st_ref = rest[-3:]
    c = pl.program_id(1)
    nb, lc, _ = q_ref.shape
    nlev = mask_ref.shape[0]
    nt = (((1,), (1,)), ((), ()))
    tn = (((0,), (0,)), ((), ()))

    @pl.when(c == 0)
    def _():
        for bi in range(nb):
            for h in range(HG_HEADS):
                if has_init:
                    st_ref[bi, h] = s0_ref[bi, h].T
                else:
                    st_ref[bi, h] = jnp.zeros((HG_DV, HG_DK), F32)

    eye = (lax.broadcasted_iota(jnp.int32, (lc, lc), 0) == lax.broadcasted_iota(jnp.int32, (lc, lc), 1))
    lvl_mask = [mask_ref[lv] > 0.5 for lv in range(nlev)]
    for bi in range(nb):
        lf = lf_ref[bi] * LOG2E
        parts, rem = [], lf
        for _ in range(LOGF_SPLIT):
            parts.append(rem.astype(BF16))
            rem = rem - parts[-1].astype(F32)
        br = jnp.dot(cmat_ref[...], jnp.concatenate(parts, 0), preferred_element_type=F32)
        for h in range(HG_HEADS):
            sl = slice(h * HG_DK, (h + 1) * HG_DK)
            bh = br[:lc, sl]
            qh = q_ref[bi, :, sl]
            kh = k_ref[bi, :, sl]
            vh = v_ref[bi, :, sl]
            a = jnp.where(eye, jnp.sum(qh * kh, -1, keepdims=True), 0.0)
            for lv in range(nlev):
                e = jnp.exp2(br[(1 + lv) * lc:(2 + lv) * lc, sl])
                a = jnp.where(lvl_mask[lv],
                              lax.dot_general((qh * e).astype(BF16), (kh * e).astype(BF16), nt,
                                              preferred_element_type=F32), a)
            st = st_ref[bi, h]
            o = (jnp.dot(a.astype(BF16), vh, preferred_element_type=F32)
                 + lax.dot_general((qh * jnp.exp2(bh)).astype(BF16), st.astype(BF16), nt,
                                   preferred_element_type=F32))
            b_last = bh[lc - 1:lc, :]
            kdec = (kh * jnp.exp2(b_last - bh)).astype(BF16)
            st_ref[bi, h] = st * jnp.exp2(b_last) + lax.dot_general(vh, kdec, tn, preferred_element_type=F32)
            on = _rms_norm(o, gn_ref[:, sl])
            o_ref[bi, :, sl] = (on * gt_ref[bi, :, sl].astype(F32)).astype(BF16)

    @pl.when(c == pl.num_programs(1) - 1)
    def _():
        for bi in range(nb):
            for h in range(HG_HEADS):
                sout_ref[bi, h] = st_ref[bi, h].T


def _hgrn(hq, hk, hlf, hv, hgt, lw, state, batch, seq, lc, nb):
    nch = seq // lc
    assert batch % nb == 0
    cmat, masks = _hgrn_consts(lc)
    has_init = state is not None
    layer = lw.layer
    row = pl.BlockSpec((nb, lc, HG_W), lambda b, c: (b, c, 0))
    st_spec = pl.BlockSpec((nb, HG_HEADS, HG_DK, HG_DV), lambda b, c: (b, 0, 0, 0))
    in_specs = [row, row, row, row, row, lw.spec("hg_norm_g"), _const_spec(cmat.shape), _const_spec(masks.shape)]
    args = [t.reshape(batch, seq, HG_W) for t in (hq, hk, hlf, hv, hgt)]
    args += [lw["hg_norm_g"], jnp.asarray(cmat, BF16), jnp.asarray(masks)]
    if has_init:
        in_specs.append(pl.BlockSpec((None, nb, HG_HEADS, HG_DK, HG_DV), lambda b, c: (layer, b, 0, 0, 0)))
        args.append(state)
    o, s_fin = pl.pallas_call(
        functools.partial(_hgrn_kernel, has_init), grid=(batch // nb, nch),
        in_specs=in_specs, out_specs=[row, st_spec],
        out_shape=[jax.ShapeDtypeStruct((batch, seq, HG_W), BF16),
                   jax.ShapeDtypeStruct((batch, HG_HEADS, HG_DK, HG_DV), F32)],
        scratch_shapes=[pltpu.VMEM((nb, HG_HEADS, HG_DV, HG_DK), F32)],
        compiler_params=_params("parallel", "arbitrary"), name="hgrn",
    )(*args)
    return o.reshape(batch * seq, HG_W), s_fin


def _merge_kernel(alpha, cml, x_ref, ain_ref, hg_ref, u_ref, vcm_ref, sga_ref, sgb_ref, sgc_ref,
                  wpa_ref, wpb_ref, wpc_ref, wo_ref, ws_ref, bs_ref, g_ref, b_ref,
                  o_ref, cin_ref):
    tm = x_ref.shape[0]
    tril = (lax.broadcasted_iota(jnp.int32, (cml, cml), 0) >= lax.broadcasted_iota(jnp.int32, (cml, cml), 1))
    for g in range(CM_GROUPS):
        sl = slice(g * CM_GROUP_DIM, (g + 1) * CM_GROUP_DIM)
        wg = jnp.where(tril, ws_ref[g, :cml, :cml], 0.0).astype(BF16)
        for r in range(tm // cml):
            rs = slice(r * cml, (r + 1) * cml)
            s = jnp.dot(wg, vcm_ref[rs, sl].astype(BF16), preferred_element_type=F32) + bs_ref[:cml, sl]
            cin_ref[rs, sl] = (u_ref[rs, sl].astype(F32) * s).astype(BF16)
    y_a = jnp.dot(ain_ref[...], wpa_ref[...], preferred_element_type=F32)
    y_b = jnp.dot(hg_ref[...], wpb_ref[...], preferred_element_type=F32)
    y_c = jnp.dot(cin_ref[...], wpc_ref[...], preferred_element_type=F32)
    mrg = (sga_ref[...].astype(F32) * y_a + sgb_ref[...].astype(F32) * y_b
           + sgc_ref[...].astype(F32) * y_c).astype(BF16)
    y = alpha * x_ref[...] + jnp.dot(mrg, wo_ref[...], preferred_element_type=F32)
    o_ref[...] = _layer_norm(y, g_ref[...], b_ref[...])


def _merge(x, a_in, hg_o, u, vcm, sga, sgb, sgc, lw, alpha, cml, tm):
    m, d = x.shape
    row = lambda w: pl.BlockSpec((tm, w), lambda i: (i, 0))
    in_specs = [row(d), row(MLA_HEADS * MLA_D_V), row(HG_W), row(CM_WIDTH), row(CM_WIDTH), row(d), row(d), row(d),
                lw.spec("w_pa"), lw.spec("w_pb"), lw.spec("w_pc"), lw.spec("w_o"),
                lw.spec("cm_ws"), lw.spec("cm_bs_full"), lw.spec("ln1_g"), lw.spec("ln1_b")]
    return pl.pallas_call(
        functools.partial(_merge_kernel, alpha, cml), grid=(m // tm,), in_specs=in_specs,
        out_specs=row(d), out_shape=jax.ShapeDtypeStruct((m, d), F32),
        scratch_shapes=[pltpu.VMEM((tm, CM_WIDTH), BF16)],
        compiler_params=_params("parallel"), name="merge",
    )(x, a_in, hg_o, u, vcm, sga, sgb, sgc, lw["w_pa"], lw["w_pb"], lw["w_pc"], lw["w_o"],
      lw["cm_ws"], lw["cm_bs_full"], lw["ln1_g"], lw["ln1_b"])


def _ffn_kernel(alpha, nsplit, x_ref, wup_ref, wdn_ref, g_ref, b_ref, o_ref):
    x = x_ref[...]
    xb = x.astype(BF16)
    dff = wdn_ref.shape[0]
    cw = dff // nsplit
    y = alpha * x
    for j in range(nsplit):
        gate = jnp.dot(xb, wup_ref[:, j * cw:(j + 1) * cw], preferred_element_type=F32)
        up = jnp.dot(xb, wup_ref[:, dff + j * cw:dff + (j + 1) * cw], preferred_element_type=F32)
        act = (_silu(gate) * up).astype(BF16)
        y = y + jnp.dot(act, wdn_ref[j * cw:(j + 1) * cw, :], preferred_element_type=F32)
    o_ref[...] = _layer_norm(y, g_ref[...], b_ref[...])


def _ffn_stream_kernel(alpha, x_ref, wg_ref, wu_ref, wdn_ref, g_ref, b_ref, o_ref, acc_ref):
    j = pl.program_id(0)

    @pl.when(j == 0)
    def _():
        acc_ref[...] = alpha * x_ref[...]

    xb = x_ref[...].astype(BF16)
    gate = jnp.dot(xb, wg_ref[...], preferred_element_type=F32)
    up = jnp.dot(xb, wu_ref[...], preferred_element_type=F32)
    acc_ref[...] += jnp.dot((_silu(gate) * up).astype(BF16), wdn_ref[...], preferred_element_type=F32)

    @pl.when(j == pl.num_programs(0) - 1)
    def _():
        o_ref[...] = _layer_norm(acc_ref[...], g_ref[...], b_ref[...])


def _ffn_stream(x, lw, alpha):
    m, d = x.shape
    layer = lw.layer
    dff = lw["w_down"].shape[1]
    cw = FFN_STREAM_CHUNK
    assert dff % cw == 0
    nch = dff // cw
    full = pl.BlockSpec((m, d), lambda j: (0, 0))
    return pl.pallas_call(
        functools.partial(_ffn_stream_kernel, alpha), grid=(nch,),
        in_specs=[full,
                  pl.BlockSpec((None, d, cw), lambda j: (layer, 0, j)),
                  pl.BlockSpec((None, d, cw), lambda j: (layer, 0, nch + j)),
                  pl.BlockSpec((None, cw, d), lambda j: (layer, j, 0)),
                  lw.spec("ln2_g"), lw.spec("ln2_b")],
        out_specs=full, out_shape=jax.ShapeDtypeStruct((m, d), F32),
        scratch_shapes=[pltpu.VMEM((m, d), F32)],
        compiler_params=_params("arbitrary"), name="ffn_stream",
    )(x, lw["w_up"], lw["w_up"], lw["w_down"], lw["ln2_g"], lw["ln2_b"])


def _ffn(x, lw, alpha, tm):
    m, d = x.shape
    if m == tm:
        return _ffn_stream(x, lw, alpha)
    row = pl.BlockSpec((tm, d), lambda i: (i, 0))
    return pl.pallas_call(
        functools.partial(_ffn_kernel, alpha, 1), grid=(m // tm,),
        in_specs=[row, lw.spec("w_up"), lw.spec("w_down"), lw.spec("ln2_g"), lw.spec("ln2_b")],
        out_specs=row, out_shape=jax.ShapeDtypeStruct((m, d), F32),
        compiler_params=_params("parallel"), name="ffn",
    )(x, lw["w_up"], lw["w_down"], lw["ln2_g"], lw["ln2_b"])


def _rope_table(pos):
    half = MLA_D_ROPE // 2
    inv = 1.0 / (ROPE_THETA ** (jnp.arange(half, dtype=F32) / half))
    ang = pos.astype(F32)[:, None] * inv[None]
    cos, sin = jnp.cos(ang), jnp.sin(ang)
    z = jnp.zeros((pos.shape[0], LANES - MLA_D_ROPE), F32)
    zh = jnp.zeros_like(sin)
    return jnp.concatenate([cos, cos, z, zh, sin, z, -sin, zh, z], -1)


def _prep_weights(w_in, q_norm_g, w_uq, w_uk, kv_norm_g, w_uv, hg_lb, hg_norm_g, cm_ln_g, cm_ln_b,
                  cm_ws, cm_bs, w_pa, w_pb, w_pc, w_o, ln1_g, ln1_b, w_up, w_down, ln2_g, ln2_b):
    depth, d, _ = w_in.shape
    o_kr = 2 * MLA_RANK
    assert w_in.shape[2] - o_kr - MLA_D_ROPE == _B_END
    w_a = w_in[:, :, :o_kr].astype(BF16)
    w_b = w_in[:, :, o_kr + MLA_D_ROPE:].astype(BF16)
    kr = w_in[:, :, o_kr:o_kr + MLA_D_ROPE]
    zk = jnp.zeros((depth, d, LANES - MLA_D_ROPE), F32)
    w_kr = jnp.concatenate([kr, zk], -1).astype(BF16)

    dq = MLA_D_NOPE + MLA_D_ROPE
    uq = w_uq.reshape(depth, MLA_RANK, MLA_HEADS, dq)
    nope, rot = uq[..., :MLA_D_NOPE], uq[..., MLA_D_NOPE:]
    zr = jnp.zeros((depth, MLA_RANK, MLA_HEADS, LANES - MLA_D_ROPE), F32)
    w_uq_r = jnp.concatenate([nope.reshape(depth, MLA_RANK, MLA_HEADS * MLA_D_NOPE),
                              jnp.concatenate([rot, zr], -1).reshape(depth, MLA_RANK, MLA_HEADS * LANES)],
                             -1).astype(BF16)
    ukt = jnp.transpose(w_uk, (0, 2, 3, 1)).reshape(depth, MLA_HEADS // 2, 2, MLA_D_NOPE, MLA_RANK)
    eye2 = jnp.eye(2, dtype=F32)
    w_uk_r = (ukt[:, :, :, :, None, :] * eye2[None, None, :, None, :, None]).reshape(
        depth, MLA_HEADS // 2, 2 * MLA_D_NOPE, 2 * MLA_RANK).astype(BF16)
    uvt = jnp.transpose(w_uv, (0, 2, 1, 3)).astype(BF16)
    lane_half = (jnp.arange(2 * MLA_D_V) // MLA_D_V)[None, None, None, :]
    head_half = (jnp.arange(MLA_HEADS) % 2)[None, :, None, None]
    w_uvp = jnp.where(lane_half == head_half, jnp.concatenate([uvt, uvt], -1), jnp.zeros((), BF16))

    sm = jax.nn.softmax(hg_lb.astype(F32), axis=0)
    lb = jnp.concatenate([jnp.zeros_like(sm[:1]), jnp.cumsum(sm[1:], axis=0)], axis=0)
    lbp = jnp.stack([jnp.log(lb + LB_TINY), jnp.log1p(-lb), 1.0 - lb], 1)

    bs_full = jnp.repeat(jnp.transpose(cm_bs, (0, 2, 1)), CM_GROUP_DIM, axis=-1)

    r3 = lambda t: t.reshape(depth, 1, -1)
    return dict(w_a=w_a, w_kr=w_kr, w_b=w_b, w_uq=w_uq_r, w_uk=w_uk_r, w_uvp=w_uvp, lbp=lbp,
                q_norm_g=r3(q_norm_g), kv_norm_g=r3(kv_norm_g), hg_norm_g=r3(hg_norm_g),
                cm_ln_g=r3(cm_ln_g), cm_ln_b=r3(cm_ln_b), cm_ws=cm_ws, cm_bs_full=bs_full,
                w_pa=w_pa.astype(BF16), w_pb=w_pb.astype(BF16), w_pc=w_pc.astype(BF16), w_o=w_o.astype(BF16),
                ln1_g=r3(ln1_g), ln1_b=r3(ln1_b), w_up=w_up.astype(BF16), w_down=w_down.astype(BF16),
                ln2_g=r3(ln2_g), ln2_b=r3(ln2_b))


def _hgrn_rows(batch):
    for nb in (8, 4, 2, 1):
        if batch % nb == 0:
            return nb


def _row_tile(m, want):
    tm = min(m, want)
    assert m % tm == 0, (m, tm)
    return tm


def _layer(x, lw, rope_tab, alpha, batch, seq, past, prev, emb, tm):
    prompt = past is None
    outs = _proj(x, lw, rope_tab, tm, BF16 if prompt else F32, prev, emb)
    if emb is not None:
        x = outs[-1]
    q, k, ckv, kr, hq, hk, hlf, hv, hgt, u, vcm, sga, sgb, sgc = outs[:14]
    if prompt:
        a_in = _attn_prompt(q, k, lw, batch, seq, Q_BLOCK)
        hg_o, s_fin = _hgrn(hq, hk, hlf, hv, hgt, lw, None, batch, seq, CHUNK, _hgrn_rows(batch))
    else:
        a_in = _attn_sample(q, k, past[0], past[1], lw, batch, seq)
        hg_o, s_fin = _hgrn(hq, hk, hlf, hv, hgt, lw, past[2], batch, seq, seq, _hgrn_rows(batch))
    cml = min(seq, CM_CHUNK)
    tm_wide = _row_tile(x.shape[0], WIDE_ROW_TILE)
    x = _merge(x, a_in, hg_o, u, vcm, sga, sgb, sgc, lw, alpha, cml, tm_wide)
    x = _ffn(x, lw, alpha, tm_wide)
    return x, (ckv, kr), (s_fin, vcm)


def kernel(x_prompt, x_sample, cache_mla_ckv, cache_mla_krope, state_hgrn, emb_ln_g, emb_ln_b, w_in, q_norm_g, w_uq, w_uk, kv_norm_g, w_uv, hg_lb, hg_norm_g, cm_ln_g, cm_ln_b, cm_ws, cm_bs, w_pa, w_pb, w_pc, w_o, ln1_g, ln1_b, w_up, w_down, ln2_g, ln2_b):
    bp, sp, d = x_prompt.shape
    bs, ss, _ = x_sample.shape
    depth = w_in.shape[0]
    past_len = cache_mla_ckv.shape[2]
    assert sp % CM_CHUNK == 0 and sp % KV_BLOCK == 0 and ss <= CHUNK and ss % 16 == 0
    alpha = float((2 * depth) ** 0.25)

    wts = _prep_weights(w_in, q_norm_g, w_uq, w_uk, kv_norm_g, w_uv, hg_lb, hg_norm_g, cm_ln_g, cm_ln_b,
                        cm_ws, cm_bs, w_pa, w_pb, w_pc, w_o, ln1_g, ln1_b, w_up, w_down, ln2_g, ln2_b)
    tm_p = _row_tile(bp * sp, ROW_TILE)
    tm_s = _row_tile(bs * ss, ROW_TILE)
    assert (sp % tm_p == 0 or tm_p % sp == 0) and (ss % tm_s == 0 or tm_s % ss == 0)
    rope_p = jnp.tile(_rope_table(jnp.arange(sp, dtype=jnp.int32)), (max(1, tm_p // sp), 1))
    rope_s = jnp.tile(_rope_table(past_len + jnp.arange(ss, dtype=jnp.int32)), (max(1, tm_s // ss), 1))
    emb = (emb_ln_g.reshape(1, d), emb_ln_b.reshape(1, d))
    xp = x_prompt.reshape(bp * sp, d)
    xs = x_sample.reshape(bs * ss, d)
    past = (cache_mla_ckv, jnp.swapaxes(cache_mla_krope, -1, -2), state_hgrn)

    stacked = lambda m: (jnp.zeros((depth, m, MLA_RANK), F32), jnp.zeros((depth, m, MLA_D_ROPE), F32))
    kv_p, kv_s = stacked(bp * sp), stacked(bs * ss)
    st_p, st_s, v_s = [], [], []
    for l in range(depth):
        lw = _LayerWeights(wts, l)
        xp, kv_p, (s_p, _) = _layer(xp, lw, rope_p, alpha, bp, sp, None, kv_p, emb if l == 0 else None, tm_p)
        xs, kv_s, (s_s, v) = _layer(xs, lw, rope_s, alpha, bs, ss, past, kv_s, emb if l == 0 else None, tm_s)
        st_p.append(s_p)
        st_s.append(s_s)
        v_s.append(v.reshape(bs, ss, -1))
    return (xp.reshape(bp, sp, d), xs.reshape(bs, ss, d),
            kv_p[0].reshape(depth, bp, sp, -1), kv_p[1].reshape(depth, bp, sp, -1), jnp.stack(st_p),
            kv_s[0].reshape(depth, bs, ss, -1), kv_s[1].reshape(depth, bs, ss, -1), jnp.stack(st_s),
            jnp.stack(v_s))
```
